```python
import jax
import jax.numpy as jnp
from jax import lax
import numpy as np

D_MODEL = 1024
BATCH = 2
SEQ = 16384
DEPTH = 4
DEC_BATCH = 16
DEC_SEQ = 32
PAST_LEN = 1024

CHUNK = 64
N_MIXERS = 3
N_HEADS = 8
HEAD_DIM = D_MODEL // N_HEADS
Q_BLOCK = 128
KEY_BLOCK = 128
D_RNN = (4 * D_MODEL // 3) // 128 * 128
N_RNN_BLOCKS = 16
RNN_BLOCK = D_RNN // N_RNN_BLOCKS
CONV_W = 4
LRU_C = 8.0
D_FF = 2 * D_MODEL
N_EXPERTS = 8
TOP_K = 2
D_FF_EXPERT = D_MODEL // 2
EPS = 1e-6
N_FOX = (DEPTH + 2) // 3
N_LRU = (DEPTH + 1) // 3
N_SB = DEPTH // 3
N_DENSE = (DEPTH + 1) // 2
N_MOE = DEPTH // 2

kernel_name = 'hybrid_stream_fox_rglru_stickbreak_step'


def rmsnorm(x, gain):
    xf = x.astype(jnp.float32)
    y = xf * lax.rsqrt(jnp.mean(xf * xf, axis=-1, keepdims=True) + EPS)
    return (y * gain.astype(jnp.float32)).astype(x.dtype)


def modulate(h, shift, scale):
    return h * (1.0 + scale[:, None, :]) + shift[:, None, :]


def causal_query_blocks(attend, q_arrays, kv_arrays, t_past):
    T = q_arrays[0].shape[1]
    outs = []
    for q0 in range(0, T, Q_BLOCK):
        qn = min(Q_BLOCK, T - q0)
        nk = t_past + q0 + qn
        qp = t_past + q0 + jnp.arange(qn)
        outs.append(attend([a[:, q0:q0 + qn] for a in q_arrays], [a[:, :nk] for a in kv_arrays], qp))
    return outs[0] if len(outs) == 1 else jnp.concatenate(outs, axis=1)


def fox_mixer(h, w_in, b_f, q_gain, k_gain, w_out, past):
    B, T, _ = h.shape
    proj = h @ w_in
    q, k, v, f_logit = jnp.split(proj, [D_MODEL, 2 * D_MODEL, 3 * D_MODEL], axis=-1)
    q = rmsnorm(q.reshape(B, T, N_HEADS, HEAD_DIM), q_gain)
    k = rmsnorm(k.reshape(B, T, N_HEADS, HEAD_DIM), k_gain)
    v = v.reshape(B, T, N_HEADS, HEAD_DIM)
    log_f = jax.nn.log_sigmoid(f_logit.astype(jnp.float32) + b_f.astype(jnp.float32))
    if past is None:
        k_all, v_all, lf_all = k, v, log_f
    else:
        pk, pv, plf = past
        k_all = jnp.concatenate([pk.astype(k.dtype), k], axis=1)
        v_all = jnp.concatenate([pv.astype(v.dtype), v], axis=1)
        lf_all = jnp.concatenate([plf.astype(jnp.float32), log_f], axis=1)
    t_past = k_all.shape[1] - T
    cum = jnp.cumsum(lf_all, axis=1)

    def attend(qs, kvs, qp):
        qb, cqb = qs
        kb, vb, ckb = kvs
        nk = kb.shape[1]
        s = jnp.einsum('bqhd,bkhd->bhqk', qb.astype(jnp.float32), kb.astype(jnp.float32)) * HEAD_DIM ** -0.5
        s = s + jnp.swapaxes(cqb, 1, 2)[..., None] - jnp.swapaxes(ckb, 1, 2)[:, :, None, :]
        s = jnp.where(jnp.arange(nk)[None, None, None, :] <= qp[None, None, :, None], s, -jnp.inf)
        p = jax.nn.softmax(s, axis=-1)
        return jnp.einsum('bhqk,bkhd->bqhd', p, vb.astype(jnp.float32))

    o = causal_query_blocks(attend, [q, cum[:, t_past:]], [k_all, v_all, cum], t_past)
    y = o.reshape(B, T, D_MODEL).astype(h.dtype) @ w_out
    return y, (k, v, log_f)


def sb_mixer(h, w_in, w_out, past):
    B, T, _ = h.shape
    q, k, v = jnp.split(h @ w_in, 3, axis=-1)
    q = q.reshape(B, T, N_HEADS, HEAD_DIM)
    k = k.reshape(B, T, N_HEADS, HEAD_DIM)
    v = v.reshape(B, T, N_HEADS, HEAD_DIM)
    if past is None:
        k_all, v_all = k, v
    else:
        pk, pv = past
        k_all = jnp.concatenate([pk.astype(k.dtype), k], axis=1)
        v_all = jnp.concatenate([pv.astype(v.dtype), v], axis=1)
    t_past = k_all.shape[1] - T
    idx = jnp.arange(KEY_BLOCK)
    tri_in = (idx[:, None] > idx[None, :]).astype(jnp.float32)

    def attend(qs, kvs, qp):
        (qb,) = qs
        kb, vb = kvs
        nk = kb.shape[1]
        pad = (-nk) % KEY_BLOCK
        if pad:
            kb = jnp.pad(kb, ((0, 0), (0, pad), (0, 0), (0, 0)))
            vb = jnp.pad(vb, ((0, 0), (0, pad), (0, 0), (0, 0)))
        nkb = (nk + pad) // KEY_BLOCK
        qn = qb.shape[1]
        z = jnp.einsum('bqhd,bkhd->bhqk', qb.astype(jnp.float32), kb.astype(jnp.float32)) * HEAD_DIM ** -0.5
        mask = jnp.arange(nk + pad)[None, None, None, :] < qp[None, None, :, None]
        log_rest = jnp.where(mask, jax.nn.log_sigmoid(-z), 0.0)
        lrb = log_rest.reshape(B, N_HEADS, qn, nkb, KEY_BLOCK)
        within = jnp.einsum('bhqnj,js->bhqns', lrb, tri_in)
        blk = jnp.arange(nkb)
        tri_blk = (blk[:, None] > blk[None, :]).astype(jnp.float32)
        later = jnp.einsum('bhqm,mn->bhqn', jnp.sum(lrb, axis=-1), tri_blk)
        between = (within + later[..., None]).reshape(B, N_HEADS, qn, nk + pad)
        a = jnp.where(mask, jnp.exp(jax.nn.log_sigmoid(z) + between), 0.0)
        return jnp.einsum('bhqk,bkhd->bqhd', a, vb.astype(jnp.float32))

    o = causal_query_blocks(attend, [q], [k_all, v_all], t_past)
    y = o.reshape(B, T, D_MODEL).astype(h.dtype) @ w_out
    return y, (k, v)


def lru_combine(e1, e2):
    a1, b1 = e1
    a2, b2 = e2
    return a1 * a2, a2 * b1 + b2


def lru_mixer(h, w_in, conv_w, conv_b, w_r, b_r, w_i, b_i, lam, w_out, past):
    B, T, _ = h.shape
    gate_branch, xr = jnp.split(h @ w_in, 2, axis=-1)
    if past is None:
        h0 = None
        prev = jnp.zeros((B, CONV_W - 1, D_RNN), xr.dtype)
    else:
        h0, prev = past
        prev = prev.astype(xr.dtype)
    xp = jnp.concatenate([prev, xr], axis=1)
    xc = conv_b
    for j in range(CONV_W):
        xc = xc + conv_w[j] * xp[:, j:j + T]
    xb = xc.reshape(B, T, N_RNN_BLOCKS, RNN_BLOCK)
    r = jax.nn.sigmoid(jnp.einsum('btnc,ncd->btnd', xb, w_r).reshape(B, T, D_RNN) + b_r)
    i = jax.nn.sigmoid(jnp.einsum('btnc,ncd->btnd', xb, w_i).reshape(B, T, D_RNN) + b_i)
    log_a = -LRU_C * r.astype(jnp.float32) * jax.nn.softplus(-lam.astype(jnp.float32))
    a = jnp.exp(log_a)
    b = jnp.sqrt(-jnp.expm1(2.0 * log_a)) * (i * xc).astype(jnp.float32)
    if h0 is not None:
        b = b.at[:, 0].add(a[:, 0] * h0.astype(jnp.float32))
    _, hs = lax.associative_scan(lru_combine, (a, b), axis=1)
    y = (jax.nn.gelu(gate_branch) * hs.astype(h.dtype)) @ w_out
    return y, (hs[:, -1], xp[:, T:])


def swiglu(h, w_gate, w_up, w_down):
    return (jax.nn.silu(h @ w_gate) * (h @ w_up)) @ w_down


def moe_ffn(h, w_router, w_gate, w_up, w_down):
    logits = (h @ w_router).astype(jnp.float32)
    top_val, top_idx = lax.top_k(logits, TOP_K)
    top_w = jax.nn.softmax(top_val, axis=-1)
    gates = jnp.sum(jax.nn.one_hot(top_idx, N_EXPERTS, dtype=jnp.float32) * top_w[..., None], axis=-2).astype(h.dtype)
    y = jnp.zeros_like(h)
    for e in range(N_EXPERTS):
        y = y + gates[..., e:e + 1] * swiglu(h, w_gate[e], w_up[e], w_down[e])
    return y


def trunk(x, c, fox_past, lru_past, sb_past, W):
    c_act = jax.nn.silu(c)
    fox_new, lru_new, sb_new = [], [], []
    for i in range(DEPTH):
        mod = c_act @ W['w_ada'][i] + W['b_ada'][i]
        shift1, scale1, gate1, shift2, scale2, gate2 = jnp.split(mod, 6, axis=-1)
        h = modulate(rmsnorm(x, W['norm_mix'][i]), shift1, scale1)
        kind, j = i % N_MIXERS, i // N_MIXERS
        if kind == 0:
            past = None if fox_past is None else (fox_past[0][j], fox_past[1][j], fox_past[2][j])
            y, st = fox_mixer(h, W['fox_w_in'][j], W['fox_b_f'][j], W['fox_q_norm'][j],
                              W['fox_k_norm'][j], W['fox_w_out'][j], past)
            fox_new.append(st)
        elif kind == 1:
            past = None if lru_past is None else (lru_past[0][j], lru_past[1][j])
            y, st = lru_mixer(h, W['lru_w_in'][j], W['lru_conv_w'][j], W['lru_conv_b'][j],
                              W['lru_w_r'][j], W['lru_b_r'][j], W['lru_w_i'][j], W['lru_b_i'][j],
                              W['lru_lambda'][j], W['lru_w_out'][j], past)
            lru_new.append(st)
        else:
            past = None if sb_past is None else (sb_past[0][j], sb_past[1][j])
            y, st = sb_mixer(h, W['sb_w_in'][j], W['sb_w_out'][j], past)
            sb_new.append(st)
        x = x + gate1[:, None, :] * y
        h = modulate(rmsnorm(x, W['norm_ffn'][i]), shift2, scale2)
        if i % 2 == 0:
            y = swiglu(h, W['ffn_w_gate'][i // 2], W['ffn_w_up'][i // 2], W['ffn_w_down'][i // 2])
        else:
            y = moe_ffn(h, W['moe_w_router'][i // 2], W['moe_w_gate'][i // 2],
                        W['moe_w_up'][i // 2], W['moe_w_down'][i // 2])
        x = x + gate2[:, None, :] * y
    fox_state = [jnp.stack([s[n] for s in fox_new]) for n in range(3)]
    lru_state = [jnp.stack([s[n] for s in lru_new]) for n in range(2)]
    sb_state = [jnp.stack([s[n] for s in sb_new]) for n in range(2)]
    return x, fox_state, lru_state, sb_state


def setup_inputs(seed: int = 0) -> dict:
    key = jax.random.key(seed)
    ks = jax.random.split(key, 40)

    def normal(n, shape, scale):
        return jax.random.normal(ks[n], shape, jnp.float32) * scale

    def uniform(n, shape, lo, hi):
        return jax.random.uniform(ks[n], shape, jnp.float32, lo, hi)

    d_in = D_MODEL ** -0.5
    kv_shape = (N_FOX, DEC_BATCH, PAST_LEN, N_HEADS, HEAD_DIM)
    sb_shape = (N_SB, DEC_BATCH, PAST_LEN, N_HEADS, HEAD_DIM)
    p_lam = uniform(27, (N_LRU, D_RNN), 0.9, 0.999) ** (1.0 / LRU_C)
    return {
        'x_prompt': normal(0, (BATCH, SEQ, D_MODEL), 1.0),
        'x_sample': normal(1, (DEC_BATCH, DEC_SEQ, D_MODEL), 1.0),
        'cache_fox_k': normal(2, kv_shape, 1.0),
        'cache_fox_v': normal(3, kv_shape, 1.0),
        'cache_fox_logf': jax.nn.log_sigmoid(normal(4, (N_FOX, DEC_BATCH, PAST_LEN, N_HEADS), 1.0) + 2.5),
        'state_lru_h': normal(5, (N_LRU, DEC_BATCH, D_RNN), 0.5),
        'state_lru_conv': normal(6, (N_LRU, DEC_BATCH, CONV_W - 1, D_RNN), 1.0),
        'cache_sb_k': normal(7, sb_shape, 1.0),
        'cache_sb_v': normal(8, sb_shape, 1.0),
        'c_prompt': normal(9, (BATCH, D_MODEL), 1.0),
        'c_sample': normal(10, (DEC_BATCH, D_MODEL), 1.0),
        'norm_mix': 1.0 + normal(11, (DEPTH, D_MODEL), 0.1),
        'norm_ffn': 1.0 + normal(12, (DEPTH, D_MODEL), 0.1),
        'w_ada': normal(13, (DEPTH, D_MODEL, 6 * D_MODEL), 0.5 * d_in),
        'b_ada': normal(14, (DEPTH, 6 * D_MODEL), 0.02),
        'fox_w_in': normal(15, (N_FOX, D_MODEL, 3 * D_MODEL + N_HEADS), d_in),
        'fox_b_f': uniform(16, (N_FOX, N_HEADS), 1.0, 4.0),
        'fox_q_norm': 1.0 + normal(17, (N_FOX, HEAD_DIM), 0.1),
        'fox_k_norm': 1.0 + normal(18, (N_FOX, HEAD_DIM), 0.1),
        'fox_w_out': normal(19, (N_FOX, D_MODEL, D_MODEL), d_in),
        'lru_w_in': normal(20, (N_LRU, D_MODEL, 2 * D_RNN), d_in),
        'lru_conv_w': normal(21, (N_LRU, CONV_W, D_RNN), 0.5),
        'lru_conv_b': normal(22, (N_LRU, D_RNN), 0.02),
        'lru_w_r': normal(23, (N_LRU, N_RNN_BLOCKS, RNN_BLOCK, RNN_BLOCK), RNN_BLOCK ** -0.5),
        'lru_b_r': normal(24, (N_LRU, D_RNN), 0.02),
        'lru_w_i': normal(25, (N_LRU, N_RNN_BLOCKS, RNN_BLOCK, RNN_BLOCK), RNN_BLOCK ** -0.5),
        'lru_b_i': normal(26, (N_LRU, D_RNN), 0.02),
        'lru_lambda': jnp.log(p_lam) - jnp.log1p(-p_lam),
        'lru_w_out': normal(28, (N_LRU, D_RNN, D_MODEL), D_RNN ** -0.5),
        'sb_w_in': normal(29, (N_SB, D_MODEL, 3 * D_MODEL), d_in),
        'sb_w_out': normal(30, (N_SB, D_MODEL, D_MODEL), d_in),
        'ffn_w_gate': normal(31, (N_DENSE, D_MODEL, D_FF), d_in),
        'ffn_w_up': normal(32, (N_DENSE, D_MODEL, D_FF), d_in),
        'ffn_w_down': normal(33, (N_DENSE, D_FF, D_MODEL), D_FF ** -0.5),
        'moe_w_router': normal(34, (N_MOE, D_MODEL, N_EXPERTS), d_in),
        'moe_w_gate': normal(35, (N_MOE, N_EXPERTS, D_MODEL, D_FF_EXPERT), d_in),
        'moe_w_up': normal(36, (N_MOE, N_EXPERTS, D_MODEL, D_FF_EXPERT), d_in),
        'moe_w_down': normal(37, (N_MOE, N_EXPERTS, D_FF_EXPERT, D_MODEL), D_FF_EXPERT ** -0.5),
    }


def reference(x_prompt, x_sample, cache_fox_k, cache_fox_v, cache_fox_logf, state_lru_h, state_lru_conv,
              cache_sb_k, cache_sb_v, c_prompt, c_sample, norm_mix, norm_ffn, w_ada, b_ada,
              fox_w_in, fox_b_f, fox_q_norm, fox_k_norm, fox_w_out,
              lru_w_in, lru_conv_w, lru_conv_b, lru_w_r, lru_b_r, lru_w_i, lru_b_i, lru_lambda, lru_w_out,
              sb_w_in, sb_w_out, ffn_w_gate, ffn_w_up, ffn_w_down,
              moe_w_router, moe_w_gate, moe_w_up, moe_w_down):
    W = {
        'norm_mix': norm_mix, 'norm_ffn': norm_ffn, 'w_ada': w_ada, 'b_ada': b_ada,
        'fox_w_in': fox_w_in, 'fox_b_f': fox_b_f, 'fox_q_norm': fox_q_norm,
        'fox_k_norm': fox_k_norm, 'fox_w_out': fox_w_out,
        'lru_w_in': lru_w_in, 'lru_conv_w': lru_conv_w, 'lru_conv_b': lru_conv_b,
        'lru_w_r': lru_w_r, 'lru_b_r': lru_b_r, 'lru_w_i': lru_w_i, 'lru_b_i': lru_b_i,
        'lru_lambda': lru_lambda, 'lru_w_out': lru_w_out,
        'sb_w_in': sb_w_in, 'sb_w_out': sb_w_out,
        'ffn_w_gate': ffn_w_gate, 'ffn_w_up': ffn_w_up, 'ffn_w_down': ffn_w_down,
        'moe_w_router': moe_w_router, 'moe_w_gate': moe_w_gate, 'moe_w_up': moe_w_up,
        'moe_w_down': moe_w_down,
    }
    y_prompt, fox_p, lru_p, sb_p = trunk(x_prompt, c_prompt, None, None, None, W)
    y_sample, fox_s, lru_s, sb_s = trunk(x_sample, c_sample,
                                         (cache_fox_k, cache_fox_v, cache_fox_logf),
                                         (state_lru_h, state_lru_conv),
                                         (cache_sb_k, cache_sb_v), W)
    return (y_prompt, y_sample,
            fox_p[0], fox_p[1], fox_p[2], fox_s[0], fox_s[1], fox_s[2],
            lru_p[0], lru_p[1], lru_s[0], lru_s[1],
            sb_p[0], sb_p[1], sb_s[0], sb_s[1])
```

```python
import functools

import jax
import jax.numpy as jnp
from jax import lax
from jax.experimental import pallas as pl
from jax.experimental.pallas import tpu as pltpu

F32 = jnp.float32
BF = jnp.bfloat16
HIGHEST = lax.Precision.HIGHEST

D_MODEL = 1024
N_HEADS = 8
HEAD_DIM = 128
N_MIXERS = 3
D_RNN = 1280
N_RNN_BLOCKS = 16
RNN_SUPER = 640
CONV_W = 4
LRU_C = 8.0
N_EXPERTS = 8
EPS = 1e-6
ATTN_SCALE = HEAD_DIM ** -0.5
LANES = 128
SUBLANES = 8
MXU_N = 256
EXP_UNDERFLOW = 104.0
VMEM_LIMIT = 56 * 2 ** 20

FOX_BLOCK = 512
SB_BLOCK = 256
ROW_TILE = 512
FFN_ROW_TILE = 1024
FFN_COL_TILE = 512


def _cparams(*sem):
    return pltpu.CompilerParams(dimension_semantics=sem, vmem_limit_bytes=VMEM_LIMIT)


def _dot(a, b, **kw):
    return jnp.dot(a, b, preferred_element_type=F32, **kw)


def _dot_nt(a, b):
    return lax.dot_general(a, b, (((1,), (1,)), ((), ())), preferred_element_type=F32)


def _normmod(x, gain, shift, scale):
    ms = jnp.mean(x * x, axis=-1, keepdims=True)
    return x * lax.rsqrt(ms + EPS) * gain * (1.0 + scale) + shift


def _softplus(x):
    return jnp.maximum(x, 0.0) + jnp.log1p(jnp.exp(-jnp.abs(x)))


def _mod_spec(mod, tm):
    d = mod.shape[-1]
    if mod.shape[1] == 1:
        return pl.BlockSpec((1, 1, d), lambda b, i, *_: (b, 0, 0))
    return pl.BlockSpec((1, tm, d), lambda b, i, *_: (b, i, 0))


def _const_spec(shape):
    zeros = (0,) * len(shape)
    return pl.BlockSpec(shape, lambda *_: zeros)


def _ada_kernel(c_ref, w_ref, b_ref, o_ref):
    c = c_ref[...]
    c_act = (c * jax.nn.sigmoid(c)).astype(BF)
    o_ref[0] = _dot(c_act, w_ref[0].astype(BF)) + b_ref[0]


def _ada(c_all, w_ada, b_ada):
    rows, d = c_all.shape
    depth, _, n = w_ada.shape
    tn = 1024
    return pl.pallas_call(
        _ada_kernel,
        grid=(depth, n // tn),
        in_specs=[pl.BlockSpec((rows, d), lambda l, j: (0, 0)),
                  pl.BlockSpec((1, d, tn), lambda l, j: (l, 0, j)),
                  pl.BlockSpec((1, 1, tn), lambda l, j: (l, 0, j))],
        out_specs=pl.BlockSpec((1, rows, tn), lambda l, j: (l, 0, j)),
        out_shape=jax.ShapeDtypeStruct((depth, rows, n), F32),
        compiler_params=_cparams("parallel", "parallel"),
    )(c_all, w_ada, b_ada.reshape(depth, 1, n))


def _qkv_kernel(*refs, fox):
    if fox:
        (x_ref, g_ref, sh_ref, sc_ref, w_ref, wf_ref, bf_ref, qg_ref, kg_ref,
         q_ref, k32_ref, kbf_ref, v32_ref, vbf_ref, lf_ref) = refs
    else:
        x_ref, g_ref, sh_ref, sc_ref, w_ref, q_ref, k32_ref, kbf_ref, v32_ref, vbf_ref = refs
    h = _normmod(x_ref[0], g_ref[...], sh_ref[0], sc_ref[0])
    hb = h.astype(BF)

    def head_norm(a, gain_ref):
        ms = jnp.mean(a * a, axis=-1, keepdims=True)
        return a * lax.rsqrt(ms + EPS) * gain_ref[...]

    for c in range(3 * D_MODEL // MXU_N):
        acc = _dot(hb, w_ref[:, c * MXU_N:(c + 1) * MXU_N])
        part = (c * MXU_N) // D_MODEL
        for hh in range(MXU_N // HEAD_DIM):
            col = (c * MXU_N) % D_MODEL + hh * HEAD_DIM
            sl = slice(col, col + HEAD_DIM)
            a = acc[:, hh * HEAD_DIM:(hh + 1) * HEAD_DIM]
            if part == 0:
                if fox:
                    a = head_norm(a, qg_ref)
                q_ref[0, :, sl] = (a * ATTN_SCALE).astype(BF)
            elif part == 1:
                if fox:
                    a = head_norm(a, kg_ref)
                k32_ref[0, :, sl] = a
                kbf_ref[0, :, sl] = a.astype(BF)
            else:
                v32_ref[0, :, sl] = a
                vbf_ref[0, :, sl] = a.astype(BF)
    if fox:
        f_logit = _dot(h, wf_ref[...], precision=HIGHEST) + bf_ref[...]
        lf_ref[0] = (-_softplus(-f_logit))[:, :N_HEADS]


def _qkv_proj(x, gain, shift, scale, w_qkv, fox_params=None):
    B, T, D = x.shape
    tm = min(T, ROW_TILE)
    fox = fox_params is not None
    row = pl.BlockSpec((1, tm, D), lambda b, i: (b, i, 0))
    in_specs = [row, _const_spec((1, D)), _mod_spec(shift, tm), _mod_spec(scale, tm), _const_spec((D, 3 * D))]
    args = [x, gain.reshape(1, D), shift, scale, w_qkv]
    out_specs = [row] * 5
    out_shape = [jax.ShapeDtypeStruct((B, T, D), dt) for dt in (BF, F32, BF, F32, BF)]
    if fox:
        w_f, b_f, q_gain, k_gain = fox_params
        in_specs += [_const_spec((D, LANES)), _const_spec((1, LANES)),
                     _const_spec((1, HEAD_DIM)), _const_spec((1, HEAD_DIM))]
        args += [jnp.pad(w_f, ((0, 0), (0, LANES - N_HEADS))),
                 jnp.pad(b_f, (0, LANES - N_HEADS)).reshape(1, LANES),
                 q_gain.reshape(1, HEAD_DIM), k_gain.reshape(1, HEAD_DIM)]
        out_specs.append(pl.BlockSpec((1, tm, N_HEADS), lambda b, i: (b, i, 0)))
        out_shape.append(jax.ShapeDtypeStruct((B, T, N_HEADS), F32))
    return pl.pallas_call(
        functools.partial(_qkv_kernel, fox=fox),
        grid=(B, T // tm),
        in_specs=in_specs, out_specs=out_specs, out_shape=out_shape,
        compiler_params=_cparams("parallel", "parallel"),
    )(*args)


def _cumsum_blocked_kernel(x_ref, o_ref):
    x = x_ref[0]
    n = x.shape[0]
    r = lax.broadcasted_iota(jnp.int32, (LANES, LANES), 0)
    c = lax.broadcasted_iota(jnp.int32, (LANES, LANES), 1)
    within = _dot(x, (r <= c).astype(F32), precision=HIGHEST)
    rn = lax.broadcasted_iota(jnp.int32, (n, n), 0)
    cn = lax.broadcasted_iota(jnp.int32, (n, n), 1)
    before = _dot((cn < rn).astype(F32), x, precision=HIGHEST)
    o_ref[0] = within + jnp.sum(before, axis=-1, keepdims=True)


def _cumsum_blocked(x):
    G, T = x.shape
    n = T // LANES
    spec = pl.BlockSpec((1, n, LANES), lambda g: (g, 0, 0))
    out = pl.pallas_call(
        _cumsum_blocked_kernel, grid=(G,), in_specs=[spec], out_specs=spec,
        out_shape=jax.ShapeDtypeStruct((G, n, LANES), F32),
        compiler_params=_cparams("parallel"),
    )(x.reshape(G, n, LANES))
    return out.reshape(G, T)


def _cumsum_flat_kernel(x_ref, o_ref):
    n = x_ref.shape[-1]
    r = lax.broadcasted_iota(jnp.int32, (n, n), 0)
    c = lax.broadcasted_iota(jnp.int32, (n, n), 1)
    o_ref[...] = _dot(x_ref[...], (r <= c).astype(F32), precision=HIGHEST)


def _cumsum_flat(x):
    return pl.pallas_call(
        _cumsum_flat_kernel, out_shape=jax.ShapeDtypeStruct(x.shape, F32),
        compiler_params=pltpu.CompilerParams(vmem_limit_bytes=VMEM_LIMIT),
    )(x)


def _fox_attn_kernel(js_ref, q_ref, k_ref, v_ref, ck_ref, o_ref, m_sc, l_sc, acc_sc, *, blk):
    b, h, i = pl.program_id(0), pl.program_id(1), pl.program_id(2)
    nq = pl.num_programs(2)
    q = q_ref[0]
    m_sc[...] = jnp.full_like(m_sc, -jnp.inf)
    l_sc[...] = jnp.zeros_like(l_sc)
    acc_sc[...] = jnp.zeros_like(acc_sc)

    def block(j, diagonal):
        start = pl.multiple_of(j * blk, blk)
        kb = k_ref[0, pl.ds(start, blk), :]
        vb = v_ref[0, pl.ds(start, blk), :]
        s = _dot_nt(q, kb) - ck_ref[0, j]
        if diagonal:
            row = lax.broadcasted_iota(jnp.int32, (blk, blk), 0)
            col = lax.broadcasted_iota(jnp.int32, (blk, blk), 1)
            s = jnp.where(col <= row, s, -jnp.inf)
        m_prev = m_sc[...]
        m_new = jnp.maximum(m_prev, jnp.max(s, axis=-1, keepdims=True))
        alpha = jnp.exp(m_prev - m_new)
        p = jnp.exp(s - m_new)
        l_sc[...] = alpha * l_sc[...] + jnp.sum(p, axis=-1, keepdims=True)
        acc_sc[...] = alpha * acc_sc[...] + _dot(p.astype(BF), vb)
        m_sc[...] = m_new

    def body(j, carry):
        block(j, False)
        return carry

    lax.fori_loop(js_ref[(b * N_HEADS + h) * nq + i], i, body, 0)
    block(i, True)
    o_ref[0] = (acc_sc[...] / l_sc[...]).astype(BF)


def _fox_first_blocks(cum, blk, logit_bound):
    G, T = cum.shape
    nb = T // blk
    cb = cum.reshape(G, nb, blk)
    prefix_min = lax.cummin(jnp.min(cb, axis=-1), axis=1)
    block_max = jnp.max(cb, axis=-1)
    dead = (prefix_min[:, None, :] - block_max[:, :, None]) > 2.0 * logit_bound + EXP_UNDERFLOW + 1.0
    j = jnp.arange(nb)
    return jnp.sum(dead & (j[None, None, :] < j[None, :, None]), axis=-1).astype(jnp.int32)


def _fox_attention(q, kbf, vbf, cum, first_blocks, blk):
    B, T, D = q.shape
    nq = T // blk
    qspec = pl.BlockSpec((1, blk, HEAD_DIM), lambda b, h, i, js: (b, i, h))
    kvspec = pl.BlockSpec((1, T, HEAD_DIM), lambda b, h, i, js: (b, 0, h))
    ckspec = pl.BlockSpec((1, nq, 1, blk), lambda b, h, i, js: (b * N_HEADS + h, 0, 0, 0))
    return pl.pallas_call(
        functools.partial(_fox_attn_kernel, blk=blk),
        grid_spec=pltpu.PrefetchScalarGridSpec(
            num_scalar_prefetch=1, grid=(B, N_HEADS, nq),
            in_specs=[qspec, kvspec, kvspec, ckspec], out_specs=qspec,
            scratch_shapes=[pltpu.VMEM((blk, 1), F32), pltpu.VMEM((blk, 1), F32),
                            pltpu.VMEM((blk, HEAD_DIM), F32)]),
        out_shape=jax.ShapeDtypeStruct((B, T, D), BF),
        compiler_params=_cparams("parallel", "parallel", "arbitrary"),
    )(first_blocks.reshape(-1), q, kbf, vbf, cum.reshape(B * N_HEADS, nq, 1, blk))


def _fox_attn_past_kernel(q_ref, kn_ref, vn_ref, kp_ref, vp_ref, cp_ref, cn_ref, o_ref):
    tn = q_ref.shape[1]
    row = lax.broadcasted_iota(jnp.int32, (tn, tn), 0)
    col = lax.broadcasted_iota(jnp.int32, (tn, tn), 1)
    for h in range(N_HEADS):
        sl = slice(h * HEAD_DIM, (h + 1) * HEAD_DIM)
        q = q_ref[0, :, sl]
        sp = _dot_nt(q, kp_ref[0, :, sl].astype(BF)) - cp_ref[0, h:h + 1, :]
        sn = _dot_nt(q, kn_ref[0, :, sl]) - cn_ref[0, h:h + 1, :]
        sn = jnp.where(col <= row, sn, -jnp.inf)
        m = jnp.maximum(jnp.max(sp, axis=-1, keepdims=True), jnp.max(sn, axis=-1, keepdims=True))
        pp = jnp.exp(sp - m)
        pn = jnp.exp(sn - m)
        l = jnp.sum(pp, axis=-1, keepdims=True) + jnp.sum(pn, axis=-1, keepdims=True)
        o = _dot(pp.astype(BF), vp_ref[0, :, sl].astype(BF)) + _dot(pn.astype(BF), vn_ref[0, :, sl])
        o_ref[0, :, sl] = (o / l).astype(BF)


def _fox_attention_past(q, kn, vn, cache_k, cache_v, layer, cum_past, cum_new):
    B, Tn, D = q.shape
    Tp = cache_k.shape[2]
    new = pl.BlockSpec((1, Tn, D), lambda b: (b, 0, 0))
    past = pl.BlockSpec((None, 1, Tp, D), lambda b: (layer, b, 0, 0))
    return pl.pallas_call(
        _fox_attn_past_kernel, grid=(B,),
        in_specs=[new, new, new, past, past,
                  pl.BlockSpec((1, N_HEADS, Tp), lambda b: (b, 0, 0)),
                  pl.BlockSpec((1, N_HEADS, Tn), lambda b: (b, 0, 0))],
        out_specs=new, out_shape=jax.ShapeDtypeStruct((B, Tn, D), BF),
        compiler_params=_cparams("parallel"),
    )(q, kn, vn, cache_k, cache_v, cum_past, cum_new)


def _split_bf16(x):
    hi = x.astype(BF)
    return hi, (x - hi.astype(F32)).astype(BF)


def _sb_attn_kernel(q_ref, k_ref, v_ref, o_ref, later_sc, acc_sc, *, blk):
    i = pl.program_id(2)
    q = q_ref[0]
    row = lax.broadcasted_iota(jnp.int32, (blk, blk), 0)
    col = lax.broadcasted_iota(jnp.int32, (blk, blk), 1)
    later_keys = (row > col).astype(BF)
    later_sc[...] = jnp.zeros_like(later_sc)
    acc_sc[...] = jnp.zeros_like(acc_sc)

    def block(j, diagonal):
        start = pl.multiple_of(j * blk, blk)
        kb = k_ref[0, pl.ds(start, blk), :]
        vb = v_ref[0, pl.ds(start, blk), :]
        z = _dot_nt(q, kb)
        log_rest = -_softplus(z)
        if diagonal:
            log_rest = jnp.where(col < row, log_rest, 0.0)
        hi, lo = _split_bf16(log_rest)
        between = _dot(hi, later_keys) + _dot(lo, later_keys) + later_sc[...]
        a = jnp.exp(z + log_rest + between)
        if diagonal:
            a = jnp.where(col < row, a, 0.0)
        acc_sc[...] += _dot(a.astype(BF), vb)
        later_sc[...] += jnp.sum(log_rest, axis=-1, keepdims=True)

    block(i, True)

    def cond(carry):
        j, live = carry
        return jnp.logical_and(j >= 0, live > -EXP_UNDERFLOW)

    def body(carry):
        j, _ = carry
        block(j, False)
        return j - 1, jnp.max(later_sc[...])

    lax.while_loop(cond, body, (i - 1, jnp.max(later_sc[...])))
    o_ref[0] = acc_sc[...].astype(BF)


def _sb_attention(q, kbf, vbf, blk):
    B, T, D = q.shape
    qspec = pl.BlockSpec((1, blk, HEAD_DIM), lambda b, h, i: (b, i, h))
    kvspec = pl.BlockSpec((1, T, HEAD_DIM), lambda b, h, i: (b, 0, h))
    return pl.pallas_call(
        functools.partial(_sb_attn_kernel, blk=blk),
        grid=(B, N_HEADS, T // blk),
        in_specs=[qspec, kvspec, kvspec], out_specs=qspec,
        scratch_shapes=[pltpu.VMEM((blk, 1), F32), pltpu.VMEM((blk, HEAD_DIM), F32)],
        out_shape=jax.ShapeDtypeStruct((B, T, D), BF),
        compiler_params=_cparams("parallel", "parallel", "arbitrary"),
    )(q, kbf, vbf)


def _sb_attn_past_kernel(q_ref, kn_ref, vn_ref, kp_ref, vp_ref, o_ref, zp_sc, lrp_sc, zn_sc, lrn_sc):
    tn = q_ref.shape[1]
    tp = kp_ref.shape[1]
    row = lax.broadcasted_iota(jnp.int32, (tn, tn), 0)
    col = lax.broadcasted_iota(jnp.int32, (tn, tn), 1)
    for h in range(N_HEADS):
        sl = slice(h * HEAD_DIM, (h + 1) * HEAD_DIM)
        rows = slice(h * tn, (h + 1) * tn)
        q = q_ref[0, :, sl]
        zp = _dot_nt(q, kp_ref[0, :, sl].astype(BF))
        zn = _dot_nt(q, kn_ref[0, :, sl])
        zp_sc[rows, :] = zp
        lrp_sc[rows, :] = -_softplus(zp)
        zn_sc[rows, :] = zn
        lrn_sc[rows, :] = jnp.where(col < row, -_softplus(zn), 0.0)
    rp = lax.broadcasted_iota(jnp.int32, (tp, tp), 0)
    cp = lax.broadcasted_iota(jnp.int32, (tp, tp), 1)
    lrp = lrp_sc[...]
    lrn = lrn_sc[...]
    hi, lo = _split_bf16(lrp)
    later_p = (rp > cp).astype(BF)
    between_p = _dot(hi, later_p) + _dot(lo, later_p) + jnp.sum(lrn, axis=-1, keepdims=True)
    hi, lo = _split_bf16(lrn)
    later_n = (row > col).astype(BF)
    between_n = _dot(hi, later_n) + _dot(lo, later_n)
    ap = jnp.exp(zp_sc[...] + lrp + between_p).astype(BF)
    an = jnp.exp(zn_sc[...] + lrn + between_n)
    for h in range(N_HEADS):
        sl = slice(h * HEAD_DIM, (h + 1) * HEAD_DIM)
        rows = slice(h * tn, (h + 1) * tn)
        a_new = jnp.where(col < row, an[rows, :], 0.0).astype(BF)
        o = _dot(ap[rows, :], vp_ref[0, :, sl].astype(BF)) + _dot(a_new, vn_ref[0, :, sl])
        o_ref[0, :, sl] = o.astype(BF)


def _sb_attention_past(q, kn, vn, cache_k, cache_v, layer):
    B, Tn, D = q.shape
    Tp = cache_k.shape[2]
    new = pl.BlockSpec((1, Tn, D), lambda b: (b, 0, 0))
    past = pl.BlockSpec((None, 1, Tp, D), lambda b: (layer, b, 0, 0))
    return pl.pallas_call(
        _sb_attn_past_kernel, grid=(B,),
        in_specs=[new, new, new, past, past], out_specs=new,
        scratch_shapes=[pltpu.VMEM((N_HEADS * Tn, Tp), F32), pltpu.VMEM((N_HEADS * Tn, Tp), F32),
                        pltpu.VMEM((N_HEADS * Tn, Tn), F32), pltpu.VMEM((N_HEADS * Tn, Tn), F32)],
        out_shape=jax.ShapeDtypeStruct((B, Tn, D), BF),
        compiler_params=_cparams("parallel"),
    )(q, kn, vn, cache_k, cache_v)


def _out_proj_kernel(a_ref, w_ref, x_ref, gate_ref, o_ref):
    o_ref[0] = x_ref[0] + gate_ref[0] * _dot(a_ref[0], w_ref[...])


def _out_proj(a, w, x, gate):
    B, T, D = x.shape
    K = a.shape[-1]
    tm = min(T, ROW_TILE)
    row = pl.BlockSpec((1, tm, D), lambda b, i: (b, i, 0))
    return pl.pallas_call(
        _out_proj_kernel, grid=(B, T // tm),
        in_specs=[pl.BlockSpec((1, tm, K), lambda b, i: (b, i, 0)), _const_spec((K, D)), row, _mod_spec(gate, tm)],
        out_specs=row, out_shape=jax.ShapeDtypeStruct((B, T, D), F32),
        compiler_params=_cparams("parallel", "parallel"),
    )(a, w, x, gate)


def _top2_gates(logits):
    lane = lax.broadcasted_iota(jnp.int32, logits.shape, 1)
    logits = jnp.where(lane < N_EXPERTS, logits, -jnp.inf)
    v1 = jnp.max(logits, axis=-1, keepdims=True)
    i1 = jnp.min(jnp.where(logits == v1, lane, LANES), axis=-1, keepdims=True)
    rest = jnp.where(lane == i1, -jnp.inf, logits)
    v2 = jnp.max(rest, axis=-1, keepdims=True)
    i2 = jnp.min(jnp.where(rest == v2, lane, LANES), axis=-1, keepdims=True)
    e2 = jnp.exp(v2 - v1)
    return jnp.where(lane == i1, 1.0 / (1.0 + e2), 0.0) + jnp.where(lane == i2, e2 / (1.0 + e2), 0.0)


def _ffn_kernel(*refs, moe):
    if moe:
        (x_ref, g_ref, sh_ref, sc_ref, gate_ref, wr_ref, wg_ref, wu_ref, wd_ref,
         o_ref, h_sc, acc_sc, gates_sc) = refs
    else:
        x_ref, g_ref, sh_ref, sc_ref, gate_ref, wg_ref, wu_ref, wd_ref, o_ref, h_sc, acc_sc = refs
    f = pl.program_id(2)

    @pl.when(f == 0)
    def _():
        h = _normmod(x_ref[0], g_ref[...], sh_ref[0], sc_ref[0])
        h_sc[...] = h.astype(BF)
        acc_sc[...] = jnp.zeros_like(acc_sc)
        if moe:
            gates_sc[...] = _top2_gates(_dot(h, wr_ref[...], precision=HIGHEST))

    hb = h_sc[...]
    g = _dot(hb, wg_ref[...])
    u = _dot(hb, wu_ref[...])
    a = g * jax.nn.sigmoid(g) * u
    if moe:
        gates = gates_sc[...]
        lane = lax.broadcasted_iota(jnp.int32, gates.shape, 1)
        a = a * jnp.sum(jnp.where(lane == f, gates, 0.0), axis=-1, keepdims=True)
    acc_sc[...] += _dot(a.astype(BF), wd_ref[...])

    @pl.when(f == pl.num_programs(2) - 1)
    def _():
        o_ref[0] = x_ref[0] + gate_ref[0] * acc_sc[...]


def _ffn(x, gain, shift, scale, gate, w_gate, w_up, w_down, w_router=None):
    B, T, D = x.shape
    tm = min(T, FFN_ROW_TILE)
    moe = w_router is not None
    row = pl.BlockSpec((1, tm, D), lambda b, i, f: (b, i, 0))
    in_specs = [row, _const_spec((1, D)), _mod_spec(shift, tm), _mod_spec(scale, tm), _mod_spec(gate, tm)]
    args = [x, gain.reshape(1, D), shift, scale, gate]
    scratch = [pltpu.VMEM((tm, D), BF), pltpu.VMEM((tm, D), F32)]
    if moe:
        nf, _, tf = w_gate.shape
        in_specs += [_const_spec((D, LANES)),
                     pl.BlockSpec((None, D, tf), lambda b, i, f: (f, 0, 0)),
                     pl.BlockSpec((None, D, tf), lambda b, i, f: (f, 0, 0)),
                     pl.BlockSpec((None, tf, D), lambda b, i, f: (f, 0, 0))]
        args.append(jnp.pad(w_router, ((0, 0), (0, LANES - N_EXPERTS))))
        scratch.append(pltpu.VMEM((tm, LANES), F32))
    else:
        tf = FFN_COL_TILE
        nf = w_gate.shape[1] // tf
        in_specs += [pl.BlockSpec((D, tf), lambda b, i, f: (0, f)),
                     pl.BlockSpec((D, tf), lambda b, i, f: (0, f)),
                     pl.BlockSpec((tf, D), lambda b, i, f: (f, 0))]
    args += [w_gate, w_up, w_down]
    return pl.pallas_call(
        functools.partial(_ffn_kernel, moe=moe),
        grid=(B, T // tm, nf),
        in_specs=in_specs, out_specs=row, scratch_shapes=scratch,
        out_shape=jax.ShapeDtypeStruct((B, T, D), F32),
        compiler_params=_cparams("parallel", "parallel", "arbitrary"),
    )(*args)


def _lru_proj_kernel(x_ref, g_ref, sh_ref, sc_ref, w_ref, gate_ref, xr_ref, tail_ref, *, tail_rows):
    h = _normmod(x_ref[0], g_ref[...], sh_ref[0], sc_ref[0])
    hb = h.astype(BF)
    tm = hb.shape[0]
    for c in range(2 * D_RNN // MXU_N):
        acc = _dot(hb, w_ref[:, c * MXU_N:(c + 1) * MXU_N])
        if c * MXU_N < D_RNN:
            gate_ref[0, :, c * MXU_N:(c + 1) * MXU_N] = jax.nn.gelu(acc).astype(BF)
        else:
            sl = slice(c * MXU_N - D_RNN, (c + 1) * MXU_N - D_RNN)
            xr_ref[0, :, sl] = acc.astype(BF)
            tail_ref[0, :, sl] = acc[tm - tail_rows:, :]


def _lru_proj(x, gain, shift, scale, w_in, all_rows_f32):
    B, T, D = x.shape
    tm = min(T, ROW_TILE)
    tail_rows = tm if all_rows_f32 else SUBLANES
    tail_spec = pl.BlockSpec((1, tail_rows, D_RNN), (lambda b, i: (b, i, 0)) if all_rows_f32
                                       else (lambda b, i: (b, 0, 0)))
    row = pl.BlockSpec((1, tm, D), lambda b, i: (b, i, 0))
    rnn = pl.BlockSpec((1, tm, D_RNN), lambda b, i: (b, i, 0))
    return pl.pallas_call(
        functools.partial(_lru_proj_kernel, tail_rows=tail_rows), grid=(B, T // tm),
        in_specs=[row, _const_spec((1, D)), _mod_spec(shift, tm), _mod_spec(scale, tm),
                  _const_spec((D, 2 * D_RNN))],
        out_specs=[rnn, rnn, tail_spec],
        out_shape=[jax.ShapeDtypeStruct((B, T, D_RNN), BF), jax.ShapeDtypeStruct((B, T, D_RNN), BF),
                   jax.ShapeDtypeStruct((B, T if all_rows_f32 else SUBLANES, D_RNN), F32)],
        compiler_params=_cparams("parallel", "arbitrary"),
    )(x, gain.reshape(1, D), shift, scale, w_in)


def _lru_core_kernel(gg_ref, xr_ref, x_ref, gate_ref, h0_ref, prev_ref, cw_ref, cb_ref, wri_ref, bri_ref,
                     lam_ref, wo_ref, o_ref, hl_ref, xp_sc, a_sc, b_sc, hs_sc, h_sc):
    i = pl.program_id(1)
    tm = xr_ref.shape[1]

    @pl.when(i == 0)
    def _():
        xp_sc[0:SUBLANES, :] = prev_ref[0]
        h_sc[...] = h0_ref[0]

    xp_sc[SUBLANES:, :] = xr_ref[0].astype(F32)
    xc = cb_ref[...]
    for j in range(CONV_W):
        off = SUBLANES - (CONV_W - 1) + j
        xc = xc + cw_ref[j:j + 1, :] * xp_sc[off:off + tm, :]
    xp_sc[0:SUBLANES, :] = xp_sc[tm:tm + SUBLANES, :]

    xcb = xc.astype(BF)
    neg_decay = -LRU_C * _softplus(-lam_ref[...])
    for s in range(D_RNN // RNN_SUPER):
        sl = slice(s * RNN_SUPER, (s + 1) * RNN_SUPER)
        z = _dot(xcb[:, sl], wri_ref[s])
        r = jax.nn.sigmoid(z[:, :RNN_SUPER] + bri_ref[0:1, sl])
        g = jax.nn.sigmoid(z[:, RNN_SUPER:] + bri_ref[1:2, sl])
        log_a = neg_decay[:, sl] * r
        y = 2.0 * log_a
        a_sc[:, sl] = jnp.exp(log_a)
        b_sc[:, sl] = jnp.sqrt(-jnp.tanh(log_a) * (jnp.exp(y) + 1.0)) * (g * xc[:, sl])

    def step(t, h):
        h = a_sc[pl.ds(t, 1), :] * h + b_sc[pl.ds(t, 1), :]
        hs_sc[pl.ds(t, 1), :] = h
        return h

    h_last = lax.fori_loop(0, tm, step, h_sc[...], unroll=8)
    h_sc[...] = h_last
    hl_ref[0] = h_last
    y = (gg_ref[0].astype(F32) * hs_sc[...]).astype(BF)
    o_ref[0] = x_ref[0] + gate_ref[0] * _dot(y, wo_ref[...])


def _block_diag(w):
    n_super = D_RNN // RNN_SUPER
    per = N_RNN_BLOCKS // n_super
    w = w.reshape(n_super, per, w.shape[1], w.shape[2])
    return jnp.einsum('sbij,bc->sbicj', w, jnp.eye(per, dtype=w.dtype)).reshape(n_super, RNN_SUPER, RNN_SUPER)


def _lru_core(gg, xr, x, gate, h0, prev, conv_w, conv_b, w_r, b_r, w_i, b_i, lam, w_out):
    B, T, D = x.shape
    tm = min(T, ROW_TILE)
    w_ri = jnp.concatenate([_block_diag(w_r), _block_diag(w_i)], axis=-1).astype(BF)
    b_ri = jnp.stack([b_r, b_i])
    row = pl.BlockSpec((1, tm, D), lambda b, i: (b, i, 0))
    rnn = pl.BlockSpec((1, tm, D_RNN), lambda b, i: (b, i, 0))
    n_super = D_RNN // RNN_SUPER
    return pl.pallas_call(
        _lru_core_kernel, grid=(B, T // tm),
        in_specs=[rnn, rnn, row, _mod_spec(gate, tm),
                  pl.BlockSpec((1, 1, D_RNN), lambda b, i: (b, 0, 0)),
                  pl.BlockSpec((1, SUBLANES, D_RNN), lambda b, i: (b, 0, 0)),
                  _const_spec((CONV_W, D_RNN)), _const_spec((1, D_RNN)),
                  _const_spec((n_super, RNN_SUPER, 2 * RNN_SUPER)), _const_spec((2, D_RNN)),
                  _const_spec((1, D_RNN)), _const_spec((D_RNN, D))],
        out_specs=[row, pl.BlockSpec((1, 1, D_RNN), lambda b, i: (b, 0, 0))],
        out_shape=[jax.ShapeDtypeStruct((B, T, D), F32), jax.ShapeDtypeStruct((B, 1, D_RNN), F32)],
        scratch_shapes=[pltpu.VMEM((tm + SUBLANES, D_RNN), F32), pltpu.VMEM((tm, D_RNN), F32),
                        pltpu.VMEM((tm, D_RNN), F32), pltpu.VMEM((tm, D_RNN), F32),
                        pltpu.VMEM((1, D_RNN), F32)],
        compiler_params=_cparams("parallel", "arbitrary"),
    )(gg, xr, x, gate, h0, prev, conv_w, conv_b.reshape(1, D_RNN), w_ri, b_ri, lam.reshape(1, D_RNN), w_out)


def _trunk(x, mod, W, past):
    B, T, D = x.shape
    depth = mod.shape[0]
    has_past = past is not None
    flat = (lambda a: a.reshape(1, B * T, a.shape[-1])) if has_past else (lambda a: a)
    unflat = (lambda a: a.reshape(B, T, a.shape[-1])) if has_past else (lambda a: a)
    fox_new, lru_new, sb_new = [], [], []
    for i in range(depth):
        per_batch = [m[:, None, :] for m in jnp.split(mod[i], 6, axis=-1)]
        if has_past:
            per_row = [jnp.broadcast_to(m, (B, T, D)).reshape(1, B * T, D) for m in per_batch]
        else:
            per_row = per_batch
        shift1, scale1, gate1, shift2, scale2, gate2 = per_row
        kind, j = i % N_MIXERS, i // N_MIXERS
        if kind == 0:
            fox_params = (W['fox_w_f'][j], W['fox_b_f'][j], W['fox_q_norm'][j], W['fox_k_norm'][j])
            q, k32, kbf, v32, vbf, logf = _qkv_proj(flat(x), W['norm_mix'][i], shift1, scale1,
                                                    W['fox_w_qkv'][j], fox_params)
            q, k32, kbf, v32, vbf, logf = (unflat(a) for a in (q, k32, kbf, v32, vbf, logf))
            lf = jnp.swapaxes(logf, 1, 2)
            if has_past:
                cache_k, cache_v, cache_lf = past['fox']
                Tp = cache_k.shape[2]
                lf_all = jnp.concatenate([jnp.swapaxes(cache_lf[j], 1, 2), lf], axis=-1)
                pad = (-lf_all.shape[-1]) % LANES
                cum = _cumsum_flat(jnp.pad(lf_all, ((0, 0), (0, 0), (0, pad))).reshape(B * N_HEADS, -1))
                cum = cum.reshape(B, N_HEADS, -1)
                o = _fox_attention_past(q, kbf, vbf, cache_k.reshape(*cache_k.shape[:3], D),
                                        cache_v.reshape(*cache_v.shape[:3], D), j,
                                        cum[..., :Tp], cum[..., Tp:Tp + T])
            else:
                cum = _cumsum_blocked(lf.reshape(B * N_HEADS, T))
                blk = min(T, FOX_BLOCK)
                bound = 1.02 * HEAD_DIM ** 0.5 * jnp.max(jnp.abs(W['fox_q_norm'][j])) \
                    * jnp.max(jnp.abs(W['fox_k_norm'][j]))
                o = _fox_attention(q, kbf, vbf, cum, _fox_first_blocks(cum, blk, bound), blk)
            x = unflat(_out_proj(flat(o), W['fox_w_out'][j], flat(x), gate1))
            fox_new.append((k32.reshape(B, T, N_HEADS, HEAD_DIM), v32.reshape(B, T, N_HEADS, HEAD_DIM), logf))
        elif kind == 1:
            gg, xr, tail = _lru_proj(flat(x), W['norm_mix'][i], shift1, scale1, W['lru_w_in'][j], has_past)
            gg, xr, tail = unflat(gg), unflat(xr), unflat(tail) if has_past else tail
            if has_past:
                state_h, state_conv = past['lru']
                h0 = state_h[j][:, None, :]
                prev = jnp.pad(state_conv[j], ((0, 0), (SUBLANES - (CONV_W - 1), 0), (0, 0)))
            else:
                h0 = jnp.zeros((B, 1, D_RNN), F32)
                prev = jnp.zeros((B, SUBLANES, D_RNN), F32)
            x, h_last = _lru_core(gg, xr, x, per_batch[2], h0, prev, W['lru_conv_w'][j], W['lru_conv_b'][j],
                                  W['lru_w_r'][j], W['lru_b_r'][j], W['lru_w_i'][j], W['lru_b_i'][j],
                                  W['lru_lambda'][j], W['lru_w_out'][j])
            lru_new.append((h_last[:, 0], tail[:, tail.shape[1] - (CONV_W - 1):]))
        else:
            q, k32, kbf, v32, vbf = _qkv_proj(flat(x), W['norm_mix'][i], shift1, scale1, W['sb_w_in'][j])
            q, k32, kbf, v32, vbf = (unflat(a) for a in (q, k32, kbf, v32, vbf))
            if has_past:
                cache_k, cache_v = past['sb']
                o = _sb_attention_past(q, kbf, vbf, cache_k.reshape(*cache_k.shape[:3], D),
                                       cache_v.reshape(*cache_v.shape[:3], D), j)
            else:
                o = _sb_attention(q, kbf, vbf, min(T, SB_BLOCK))
            x = unflat(_out_proj(flat(o), W['sb_w_out'][j], flat(x), gate1))
            sb_new.append((k32.reshape(B, T, N_HEADS, HEAD_DIM), v32.reshape(B, T, N_HEADS, HEAD_DIM)))
        if i % 2 == 0:
            x = unflat(_ffn(flat(x), W['norm_ffn'][i], shift2, scale2, gate2,
                            W['ffn_w_gate'][i // 2], W['ffn_w_up'][i // 2], W['ffn_w_down'][i // 2]))
        else:
            x = unflat(_ffn(flat(x), W['norm_ffn'][i], shift2, scale2, gate2,
                            W['moe_w_gate'][i // 2], W['moe_w_up'][i // 2], W['moe_w_down'][i // 2],
                            W['moe_w_router'][i // 2]))
    return x, fox_new, lru_new, sb_new


MATMUL_WEIGHTS = ('fox_w_qkv', 'fox_w_out', 'lru_w_in', 'lru_w_out', 'sb_w_in', 'sb_w_out',
                  'ffn_w_gate', 'ffn_w_up', 'ffn_w_down', 'moe_w_gate', 'moe_w_up', 'moe_w_down')


def _prepare_weights(W):
    W = dict(W)
    w_in = W.pop('fox_w_in')
    W['fox_w_qkv'] = w_in[:, :, :3 * D_MODEL]
    W['fox_w_f'] = w_in[:, :, 3 * D_MODEL:]
    for name in MATMUL_WEIGHTS:
        W[name] = W[name].astype(BF)
    return W


def _forward(W, x_prompt, c_prompt, x_sample, c_sample, past):
    W = _prepare_weights(W)
    n_prompt, n_sample = c_prompt.shape[0], c_sample.shape[0]
    c_all = jnp.concatenate([c_prompt, c_sample], axis=0)
    c_all = jnp.pad(c_all, ((0, (-c_all.shape[0]) % (2 * SUBLANES)), (0, 0)))
    mod = _ada(c_all, W['w_ada'], W['b_ada'])
    y_p, fox_p, lru_p, sb_p = _trunk(x_prompt, mod[:, :n_prompt], W, None)
    y_s, fox_s, lru_s, sb_s = _trunk(x_sample, mod[:, n_prompt:n_prompt + n_sample], W, past)

    def stacked(states):
        return [jnp.stack([s[n] for s in states]) for n in range(len(states[0]))]

    return (y_p, y_s, *stacked(fox_p), *stacked(fox_s), *stacked(lru_p), *stacked(lru_s),
            *stacked(sb_p), *stacked(sb_s))


def kernel(x_prompt, x_sample, cache_fox_k, cache_fox_v, cache_fox_logf, state_lru_h, state_lru_conv,
           cache_sb_k, cache_sb_v, c_prompt, c_sample, norm_mix, norm_ffn, w_ada, b_ada,
           fox_w_in, fox_b_f, fox_q_norm, fox_k_norm, fox_w_out,
           lru_w_in, lru_conv_w, lru_conv_b, lru_w_r, lru_b_r, lru_w_i, lru_b_i, lru_lambda, lru_w_out,
           sb_w_in, sb_w_out, ffn_w_gate, ffn_w_up, ffn_w_down,
           moe_w_router, moe_w_gate, moe_w_up, moe_w_down):
    W = {
        'norm_mix': norm_mix, 'norm_ffn': norm_ffn, 'w_ada': w_ada, 'b_ada': b_ada,
        'fox_w_in': fox_w_in, 'fox_b_f': fox_b_f, 'fox_q_norm': fox_q_norm,
        'fox_k_norm': fox_k_norm, 'fox_w_out': fox_w_out,
        'lru_w_in': lru_w_in, 'lru_conv_w': lru_conv_w, 'lru_conv_b': lru_conv_b,
        'lru_w_r': lru_w_r, 'lru_b_r': lru_b_r, 'lru_w_i': lru_w_i, 'lru_b_i': lru_b_i,
        'lru_lambda': lru_lambda, 'lru_w_out': lru_w_out,
        'sb_w_in': sb_w_in, 'sb_w_out': sb_w_out,
        'ffn_w_gate': ffn_w_gate, 'ffn_w_up': ffn_w_up, 'ffn_w_down': ffn_w_down,
        'moe_w_router': moe_w_router, 'moe_w_gate': moe_w_gate, 'moe_w_up': moe_w_up,
        'moe_w_down': moe_w_down,
    }
    past = {'fox': (cache_fox_k, cache_fox_v, cache_fox_logf), 'lru': (state_lru_h, state_lru_conv),
            'sb': (cache_sb_k, cache_sb_v)}
    return _forward(W, x_prompt, c_prompt, x_sample, c_sample, past)
```

```python
import functools

import jax
import jax.numpy as jnp
from jax import lax
from jax.experimental import pallas as pl
from jax.experimental.pallas import tpu as pltpu

F32 = jnp.float32
BF = jnp.bfloat16
HIGHEST = lax.Precision.HIGHEST

D_MODEL = 1024
N_HEADS = 8
HEAD_DIM = 128
N_MIXERS = 3
D_RNN = 1280
N_RNN_BLOCKS = 16
RNN_SUPER = 640
CONV_W = 4
LRU_C = 8.0
N_EXPERTS = 8
EPS = 1e-6
ATTN_SCALE = HEAD_DIM ** -0.5
LANES = 128
SUBLANES = 8
MXU_N = 256
EXP_UNDERFLOW = 104.0
EXP2_UNDERFLOW = 150.0
LOG2E = 1.4426950408889634
Q_SCALE = ATTN_SCALE * LOG2E
FIXED_REF_MAX_BOUND = 30.0
VMEM_LIMIT = 56 * 2 ** 20

FOX_BLOCK = 512
SB_BLOCK = 256
ROW_TILE = 512
FFN_ROW_TILE = 1024
FFN_COL_TILE = 512


def _cparams(*sem):
    return pltpu.CompilerParams(dimension_semantics=sem, vmem_limit_bytes=VMEM_LIMIT)


def _dot(a, b, **kw):
    return jnp.dot(a, b, preferred_element_type=F32, **kw)


def _dot_nt(a, b):
    return lax.dot_general(a, b, (((1,), (1,)), ((), ())), preferred_element_type=F32)


def _normmod(x, gain, shift, scale):
    ms = jnp.mean(x * x, axis=-1, keepdims=True)
    return x * lax.rsqrt(ms + EPS) * gain * (1.0 + scale) + shift


def _softplus(x):
    return jnp.maximum(x, 0.0) + jnp.log1p(jnp.exp(-jnp.abs(x)))


def _mod_spec(mod, tm):
    d = mod.shape[-1]
    if mod.shape[1] == 1:
        return pl.BlockSpec((1, 1, d), lambda b, i, *_: (b, 0, 0))
    return pl.BlockSpec((1, tm, d), lambda b, i, *_: (b, i, 0))


def _const_spec(shape):
    zeros = (0,) * len(shape)
    return pl.BlockSpec(shape, lambda *_: zeros)


def _ada_kernel(c_ref, w_ref, b_ref, o_ref):
    c = c_ref[...]
    c_act = (c * jax.nn.sigmoid(c)).astype(BF)
    o_ref[0] = _dot(c_act, w_ref[0].astype(BF)) + b_ref[0]


def _ada(c_all, w_ada, b_ada):
    rows, d = c_all.shape
    depth, _, n = w_ada.shape
    tn = 1024
    return pl.pallas_call(
        _ada_kernel,
        grid=(depth, n // tn),
        in_specs=[pl.BlockSpec((rows, d), lambda l, j: (0, 0)),
                  pl.BlockSpec((1, d, tn), lambda l, j: (l, 0, j)),
                  pl.BlockSpec((1, 1, tn), lambda l, j: (l, 0, j))],
        out_specs=pl.BlockSpec((1, rows, tn), lambda l, j: (l, 0, j)),
        out_shape=jax.ShapeDtypeStruct((depth, rows, n), F32),
        compiler_params=_cparams("parallel", "parallel"),
        name="ada_modulation",
    )(c_all, w_ada, b_ada.reshape(depth, 1, n))


def _qkv_kernel(*refs, fox, aliased):
    refs = list(refs)
    x_ref, g_ref, sh_ref, sc_ref, w_ref = refs[:5]
    del refs[:5]
    if fox:
        wf_ref, bf_ref, qg_ref, kg_ref = refs[:4]
        del refs[:4]
    if aliased:
        del refs[:2]
    q_ref, k32_ref, kbf_ref, v32_ref, vbf_ref = refs[:5]
    h = _normmod(x_ref[0], g_ref[...], sh_ref[0], sc_ref[0])
    hb = h.astype(BF)
    tm = hb.shape[0]

    def head_norm(a, gain_ref):
        ms = jnp.mean(a * a, axis=-1, keepdims=True)
        return a * lax.rsqrt(ms + EPS) * gain_ref[...]

    for c in range(3 * D_MODEL // MXU_N):
        acc = _dot(hb, w_ref[:, c * MXU_N:(c + 1) * MXU_N])
        part = (c * MXU_N) // D_MODEL
        for hh in range(MXU_N // HEAD_DIM):
            col = (c * MXU_N) % D_MODEL + hh * HEAD_DIM
            sl = slice(col, col + HEAD_DIM)
            state_rows = pl.ds(col // HEAD_DIM, tm, stride=N_HEADS)
            a = acc[:, hh * HEAD_DIM:(hh + 1) * HEAD_DIM]
            if part == 0:
                if fox:
                    a = head_norm(a, qg_ref)
                q_ref[0, :, sl] = (a * Q_SCALE).astype(BF)
            elif part == 1:
                if fox:
                    a = head_norm(a, kg_ref)
                k32_ref[0, state_rows, :] = a
                kbf_ref[0, :, sl] = a.astype(BF)
            else:
                v32_ref[0, state_rows, :] = a
                vbf_ref[0, :, sl] = a.astype(BF)
    if fox:
        f_logit = _dot(h, wf_ref[...], precision=HIGHEST) + bf_ref[...]
        refs[5][0] = -_softplus(-f_logit)


def _qkv_proj(x, gain, shift, scale, w_qkv, fox_params=None, n_slots=1, slot=0, state=None):
    B, T, D = x.shape
    tm = min(T, ROW_TILE)
    fox = fox_params is not None
    aliased = state is not None
    row = pl.BlockSpec((1, tm, D), lambda b, i: (b, i, 0))
    state_spec = pl.BlockSpec((None, 1, tm * N_HEADS, HEAD_DIM), lambda b, i: (slot, b, i, 0))
    state_shape = jax.ShapeDtypeStruct((n_slots, B, T * N_HEADS, HEAD_DIM), F32)
    in_specs = [row, _const_spec((1, D)), _mod_spec(shift, tm), _mod_spec(scale, tm), _const_spec((D, 3 * D))]
    args = [x, gain.reshape(1, D), shift, scale, w_qkv]
    out_specs = [row, state_spec, row, state_spec, row]
    out_shape = [jax.ShapeDtypeStruct((B, T, D), BF), state_shape, jax.ShapeDtypeStruct((B, T, D), BF),
                 state_shape, jax.ShapeDtypeStruct((B, T, D), BF)]
    if fox:
        w_f, b_f, q_gain, k_gain = fox_params
        in_specs += [_const_spec((D, LANES)), _const_spec((1, LANES)),
                     _const_spec((1, HEAD_DIM)), _const_spec((1, HEAD_DIM))]
        args += [jnp.pad(w_f, ((0, 0), (0, LANES - N_HEADS))),
                 jnp.pad(b_f, (0, LANES - N_HEADS)).reshape(1, LANES),
                 q_gain.reshape(1, HEAD_DIM), k_gain.reshape(1, HEAD_DIM)]
        out_specs.append(pl.BlockSpec((1, tm, LANES), lambda b, i: (b, i, 0)))
        out_shape.append(jax.ShapeDtypeStruct((B, T, LANES), F32))
    aliases = {}
    if aliased:
        aliases = {len(args): 1, len(args) + 1: 3}
        in_specs += [pl.BlockSpec(memory_space=pl.ANY)] * 2
        args += list(state)
    return pl.pallas_call(
        functools.partial(_qkv_kernel, fox=fox, aliased=aliased),
        grid=(B, T // tm),
        in_specs=in_specs, out_specs=out_specs, out_shape=out_shape,
        input_output_aliases=aliases,
        compiler_params=_cparams("parallel", "parallel"),
        name="fox_qkv_proj" if fox else "sb_qkv_proj",
    )(*args)


def _split3_bf16(x):
    hi = x.astype(BF).astype(F32)
    mid = (x - hi).astype(BF).astype(F32)
    return hi, mid, (x - hi - mid).astype(BF).astype(F32)


def _fox_prep_kernel(bound_ref, lf_ref, qx_ref, kx_ref, cum_ref, bmin_ref, bmax_ref, carry_sc):
    i = pl.program_id(1)
    tm = lf_ref.shape[1]

    @pl.when(i == 0)
    def _():
        carry_sc[...] = jnp.zeros_like(carry_sc)

    r = lax.broadcasted_iota(jnp.int32, (tm, tm), 0)
    c = lax.broadcasted_iota(jnp.int32, (tm, tm), 1)
    cum = _dot((c <= r).astype(F32), lf_ref[0], precision=HIGHEST) + carry_sc[...]
    carry_sc[...] = cum[tm - 1:tm, :]
    cum_ref[0] = cum
    bmin_ref[0, 0] = jnp.min(cum, axis=0, keepdims=True)
    bmax_ref[0, 0] = jnp.max(cum, axis=0, keepdims=True)
    c2 = cum * LOG2E
    bound2 = bound_ref[0] * LOG2E
    lane = lax.broadcasted_iota(jnp.int32, (tm, LANES), 1)
    for h in range(N_HEADS):
        col = c2[:, h:h + 1]
        k1, k2, k3 = _split3_bf16(-col)
        q1, q2, q3 = _split3_bf16(col - bound2)
        kx = jnp.where(lane == 0, k1, jnp.where(lane == 1, k2, jnp.where(lane == 2, k3,
                                                                         jnp.where(lane < 6, 1.0, 0.0))))
        qx = jnp.where(lane < 3, 1.0, jnp.where(lane == 3, q1, jnp.where(lane == 4, q2,
                                                                         jnp.where(lane == 5, q3, 0.0))))
        sl = slice(h * LANES, (h + 1) * LANES)
        qx_ref[0, :, sl] = qx.astype(BF)
        kx_ref[0, :, sl] = kx.astype(BF)


def _fox_prep(logf, bound, blk):
    B, T, _ = logf.shape
    nb = T // blk
    row = pl.BlockSpec((1, blk, LANES), lambda b, i: (b, i, 0))
    wide = pl.BlockSpec((1, blk, N_HEADS * LANES), lambda b, i: (b, i, 0))
    stat = pl.BlockSpec((1, 1, 1, LANES), lambda b, i: (b, i, 0, 0))
    return pl.pallas_call(
        _fox_prep_kernel, grid=(B, nb),
        in_specs=[pl.BlockSpec(memory_space=pltpu.SMEM), row],
        out_specs=[wide, wide, row, stat, stat],
        out_shape=[jax.ShapeDtypeStruct((B, T, N_HEADS * LANES), BF)] * 2
        + [jax.ShapeDtypeStruct((B, T, LANES), F32)] + [jax.ShapeDtypeStruct((B, nb, 1, LANES), F32)] * 2,
        scratch_shapes=[pltpu.VMEM((1, LANES), F32)],
        compiler_params=_cparams("parallel", "arbitrary"),
        name="fox_prep",
    )(bound.reshape(1), logf)


def _cumsum_flat_kernel(x_ref, o_ref):
    n = x_ref.shape[-1]
    r = lax.broadcasted_iota(jnp.int32, (n, n), 0)
    c = lax.broadcasted_iota(jnp.int32, (n, n), 1)
    o_ref[...] = _dot(x_ref[...], (r <= c).astype(F32), precision=HIGHEST)


def _cumsum_flat(x):
    return pl.pallas_call(
        _cumsum_flat_kernel, out_shape=jax.ShapeDtypeStruct(x.shape, F32),
        compiler_params=pltpu.CompilerParams(vmem_limit_bytes=VMEM_LIMIT),
        name="cumsum_flat",
    )(x)


def _fox_attn_kernel(js_ref, q_ref, qx_ref, k_ref, kx_ref, v_ref, o_ref, acc_sc, *, blk):
    b, h, i = pl.program_id(0), pl.program_id(1), pl.program_id(2)
    nq = pl.num_programs(2)
    q = jnp.concatenate([q_ref[0], qx_ref[0]], axis=-1)
    acc_sc[...] = jnp.zeros_like(acc_sc)
    lane = lax.broadcasted_iota(jnp.int32, (blk, HEAD_DIM), 1)
    ones_col = (lane == 0).astype(BF)

    def block(j, diagonal):
        start = pl.multiple_of(j * blk, blk)
        kb = jnp.concatenate([k_ref[0, pl.ds(start, blk), :], kx_ref[0, pl.ds(start, blk), :]], axis=-1)
        vb = jnp.concatenate([v_ref[0, pl.ds(start, blk), :], ones_col], axis=-1)
        s = _dot_nt(q, kb)
        if diagonal:
            row = lax.broadcasted_iota(jnp.int32, (blk, blk), 0)
            col = lax.broadcasted_iota(jnp.int32, (blk, blk), 1)
            s = jnp.where(col <= row, s, -jnp.inf)
        acc_sc[...] += _dot(jnp.exp2(s).astype(BF), vb)

    j0 = js_ref[(b * N_HEADS + h) * nq + i]
    n_before = i - j0

    def pair(t, carry):
        block(j0 + 2 * t, False)
        block(j0 + 2 * t + 1, False)
        return carry

    lax.fori_loop(0, n_before // 2, pair, 0)

    @pl.when(n_before % 2 == 1)
    def _():
        block(i - 1, False)
        block(i, True)

    @pl.when(n_before % 2 == 0)
    def _():
        block(i, True)

    acc = acc_sc[...]
    o_ref[0] = (acc[:, :HEAD_DIM] / acc[:, HEAD_DIM:HEAD_DIM + 1]).astype(BF)


def _fox_attn_online_kernel(js_ref, q_ref, k_ref, v_ref, ck_ref, o_ref, m_sc, l_sc, acc_sc, *, blk):
    b, h, i = pl.program_id(0), pl.program_id(1), pl.program_id(2)
    nq = pl.num_programs(2)
    q = q_ref[0]
    m_sc[...] = jnp.full_like(m_sc, -jnp.inf)
    l_sc[...] = jnp.zeros_like(l_sc)
    acc_sc[...] = jnp.zeros_like(acc_sc)

    def block(j, diagonal):
        start = pl.multiple_of(j * blk, blk)
        kb = k_ref[0, pl.ds(start, blk), :]
        vb = v_ref[0, pl.ds(start, blk), :]
        s = _dot_nt(q, kb) - ck_ref[0, j]
        if diagonal:
            row = lax.broadcasted_iota(jnp.int32, (blk, blk), 0)
            col = lax.broadcasted_iota(jnp.int32, (blk, blk), 1)
            s = jnp.where(col <= row, s, -jnp.inf)
        m_prev = m_sc[...]
        m_new = jnp.maximum(m_prev, jnp.max(s, axis=-1, keepdims=True))
        alpha = jnp.exp2(m_prev - m_new)
        p = jnp.exp2(s - m_new)
        l_sc[...] = alpha * l_sc[...] + jnp.sum(p, axis=-1, keepdims=True)
        acc_sc[...] = alpha * acc_sc[...] + _dot(p.astype(BF), vb)
        m_sc[...] = m_new

    def body(j, carry):
        block(j, False)
        return carry

    lax.fori_loop(js_ref[(b * N_HEADS + h) * nq + i], i, body, 0)
    block(i, True)
    o_ref[0] = (acc_sc[...] / l_sc[...]).astype(BF)


def _fox_first_blocks(block_min, block_max, logit_bound):
    nb = block_min.shape[-1]
    j = jnp.arange(nb)
    upto = j[None, :] <= j[:, None]
    prefix_min = jnp.min(jnp.where(upto, block_min[..., None, :], jnp.inf), axis=-1)
    dead = (prefix_min[..., None, :] - block_max[..., :, None]) > 2.0 * logit_bound + EXP_UNDERFLOW + 1.0
    return jnp.sum(dead & (j[None, :] < j[:, None]), axis=-1).astype(jnp.int32)


def _fox_attention(q, kbf, vbf, logf, logit_bound, blk):
    B, T, D = q.shape
    nq = T // blk
    qx, kx, cum, bmin, bmax = _fox_prep(logf, logit_bound, blk)
    heads_first = lambda a: jnp.swapaxes(a[:, :, 0, :N_HEADS], 1, 2)
    first_blocks = _fox_first_blocks(heads_first(bmin), heads_first(bmax), logit_bound).reshape(-1)
    qspec = pl.BlockSpec((1, blk, HEAD_DIM), lambda b, h, i, js: (b, i, h))
    kvspec = pl.BlockSpec((1, T, HEAD_DIM), lambda b, h, i, js: (b, 0, h))
    out_shape = jax.ShapeDtypeStruct((B, T, D), BF)

    def fixed_reference():
        return pl.pallas_call(
            functools.partial(_fox_attn_kernel, blk=blk),
            grid_spec=pltpu.PrefetchScalarGridSpec(
                num_scalar_prefetch=1, grid=(B, N_HEADS, nq),
                in_specs=[qspec, qspec, kvspec, kvspec, kvspec], out_specs=qspec,
                scratch_shapes=[pltpu.VMEM((blk, 2 * HEAD_DIM), F32)]),
            out_shape=out_shape,
            compiler_params=_cparams("parallel", "parallel", "arbitrary"),
            name="fox_attention",
        )(first_blocks, q, qx, kbf, kx, vbf)

    def online():
        ck = jnp.swapaxes(cum[:, :, :N_HEADS], 1, 2) * LOG2E
        ckspec = pl.BlockSpec((1, nq, 1, blk), lambda b, h, i, js: (b * N_HEADS + h, 0, 0, 0))
        return pl.pallas_call(
            functools.partial(_fox_attn_online_kernel, blk=blk),
            grid_spec=pltpu.PrefetchScalarGridSpec(
                num_scalar_prefetch=1, grid=(B, N_HEADS, nq),
                in_specs=[qspec, kvspec, kvspec, ckspec], out_specs=qspec,
                scratch_shapes=[pltpu.VMEM((blk, 1), F32), pltpu.VMEM((blk, 1), F32),
                                pltpu.VMEM((blk, HEAD_DIM), F32)]),
            out_shape=out_shape,
            compiler_params=_cparams("parallel", "parallel", "arbitrary"),
            name="fox_attention_online",
        )(first_blocks, q, kbf, vbf, ck.reshape(B * N_HEADS, nq, 1, blk))

    return lax.cond(logit_bound < FIXED_REF_MAX_BOUND, fixed_reference, online)


def _fox_attn_past_kernel(q_ref, kn_ref, vn_ref, kp_ref, vp_ref, cp_ref, cn_ref, o_ref):
    tn = q_ref.shape[1]
    row = lax.broadcasted_iota(jnp.int32, (tn, tn), 0)
    col = lax.broadcasted_iota(jnp.int32, (tn, tn), 1)
    for h in range(N_HEADS):
        sl = slice(h * HEAD_DIM, (h + 1) * HEAD_DIM)
        q = q_ref[0, :, sl]
        sp = _dot_nt(q, kp_ref[0, :, sl].astype(BF)) - cp_ref[0, h:h + 1, :]
        sn = _dot_nt(q, kn_ref[0, :, sl]) - cn_ref[0, h:h + 1, :]
        sn = jnp.where(col <= row, sn, -jnp.inf)
        m = jnp.maximum(jnp.max(sp, axis=-1, keepdims=True), jnp.max(sn, axis=-1, keepdims=True))
        pp = jnp.exp2(sp - m)
        pn = jnp.exp2(sn - m)
        l = jnp.sum(pp, axis=-1, keepdims=True) + jnp.sum(pn, axis=-1, keepdims=True)
        o = _dot(pp.astype(BF), vp_ref[0, :, sl].astype(BF)) + _dot(pn.astype(BF), vn_ref[0, :, sl])
        o_ref[0, :, sl] = (o / l).astype(BF)


def _fox_attention_past(q, kn, vn, cache_k, cache_v, layer, cum_past, cum_new):
    B, Tn, D = q.shape
    Tp = cache_k.shape[2]
    new = pl.BlockSpec((1, Tn, D), lambda b: (b, 0, 0))
    past = pl.BlockSpec((None, 1, Tp, D), lambda b: (layer, b, 0, 0))
    return pl.pallas_call(
        _fox_attn_past_kernel, grid=(B,),
        in_specs=[new, new, new, past, past,
                  pl.BlockSpec((1, N_HEADS, Tp), lambda b: (b, 0, 0)),
                  pl.BlockSpec((1, N_HEADS, Tn), lambda b: (b, 0, 0))],
        out_specs=new, out_shape=jax.ShapeDtypeStruct((B, Tn, D), BF),
        compiler_params=_cparams("parallel"),
        name="fox_attention_past",
    )(q, kn, vn, cache_k, cache_v, cum_past, cum_new)


def _split_bf16(x):
    hi = x.astype(BF)
    return hi, (x - hi.astype(F32)).astype(BF)


def _neg_log2_rest(z2):
    zc = jnp.minimum(z2, 30.0)
    return jnp.log(1.0 + jnp.exp2(zc)) * LOG2E + (z2 - zc)


def _sb_attn_kernel(q_ref, k_ref, v_ref, o_ref, later_sc, acc_sc, *, blk):
    i = pl.program_id(2)
    q = q_ref[0]
    row = lax.broadcasted_iota(jnp.int32, (blk, blk), 0)
    col = lax.broadcasted_iota(jnp.int32, (blk, blk), 1)
    suffix = jnp.where(row >= col, -1.0, 0.0).astype(BF)
    suffix2 = jnp.concatenate([suffix, suffix], axis=0)

    def scores(j, diagonal):
        start = pl.multiple_of(j * blk, blk)
        z2 = _dot_nt(q, k_ref[0, pl.ds(start, blk), :])
        nlr = _neg_log2_rest(z2)
        if diagonal:
            nlr = jnp.where(col < row, nlr, 0.0)
        hi, lo = _split_bf16(nlr)
        tail = _dot(jnp.concatenate([hi, lo], axis=-1), suffix2)
        return z2 + tail, tail[:, 0:1]

    def weighted(e, later, j, diagonal):
        a = jnp.exp2(e + later)
        if diagonal:
            a = jnp.where(col < row, a, 0.0)
        start = pl.multiple_of(j * blk, blk)
        return _dot(a.astype(BF), v_ref[0, pl.ds(start, blk), :])

    @pl.when(i == 0)
    def _():
        e, total = scores(0, True)
        acc_sc[...] = weighted(e, 0.0, 0, True)
        later_sc[...] = total

    @pl.when(i > 0)
    def _():
        e_d, total_d = scores(i, True)
        e_p, total_p = scores(i - 1, False)
        acc_sc[...] = weighted(e_d, 0.0, i, True) + weighted(e_p, total_d, i - 1, False)
        later_sc[...] = total_d + total_p

    def cond(carry):
        j, live = carry
        return jnp.logical_and(j >= 0, live > -EXP2_UNDERFLOW)

    def body(carry):
        j, _ = carry
        e, total = scores(j, False)
        later = later_sc[...]
        acc_sc[...] += weighted(e, later, j, False)
        later_sc[...] = later + total
        return j - 1, jnp.max(later_sc[...])

    lax.while_loop(cond, body, (i - 2, jnp.max(later_sc[...])))
    o_ref[0] = acc_sc[...].astype(BF)


def _sb_attention(q, kbf, vbf, blk):
    B, T, D = q.shape
    qspec = pl.BlockSpec((1, blk, HEAD_DIM), lambda b, h, i: (b, i, h))
    kvspec = pl.BlockSpec((1, T, HEAD_DIM), lambda b, h, i: (b, 0, h))
    return pl.pallas_call(
        functools.partial(_sb_attn_kernel, blk=blk),
        grid=(B, N_HEADS, T // blk),
        in_specs=[qspec, kvspec, kvspec], out_specs=qspec,
        scratch_shapes=[pltpu.VMEM((blk, 1), F32), pltpu.VMEM((blk, HEAD_DIM), F32)],
        out_shape=jax.ShapeDtypeStruct((B, T, D), BF),
        compiler_params=_cparams("parallel", "parallel", "arbitrary"),
        name="sb_attention",
    )(q, kbf, vbf)


def _sb_attn_past_kernel(q_ref, kn_ref, vn_ref, kp_ref, vp_ref, o_ref, zp_sc, lrp_sc, zn_sc, lrn_sc):
    tn = q_ref.shape[1]
    tp = kp_ref.shape[1]
    row = lax.broadcasted_iota(jnp.int32, (tn, tn), 0)
    col = lax.broadcasted_iota(jnp.int32, (tn, tn), 1)
    for h in range(N_HEADS):
        sl = slice(h * HEAD_DIM, (h + 1) * HEAD_DIM)
        rows = slice(h * tn, (h + 1) * tn)
        q = q_ref[0, :, sl]
        zp = _dot_nt(q, kp_ref[0, :, sl].astype(BF))
        zn = _dot_nt(q, kn_ref[0, :, sl])
        zp_sc[rows, :] = zp
        lrp_sc[rows, :] = -_neg_log2_rest(zp)
        zn_sc[rows, :] = zn
        lrn_sc[rows, :] = jnp.where(col < row, -_neg_log2_rest(zn), 0.0)
    rp = lax.broadcasted_iota(jnp.int32, (tp, tp), 0)
    cp = lax.broadcasted_iota(jnp.int32, (tp, tp), 1)
    lrp = lrp_sc[...]
    lrn = lrn_sc[...]
    hi, lo = _split_bf16(lrp)
    later_p = (rp > cp).astype(BF)
    between_p = _dot(hi, later_p) + _dot(lo, later_p) + jnp.sum(lrn, axis=-1, keepdims=True)
    hi, lo = _split_bf16(lrn)
    later_n = (row > col).astype(BF)
    between_n = _dot(hi, later_n) + _dot(lo, later_n)
    ap = jnp.exp2(zp_sc[...] + lrp + between_p).astype(BF)
    an = jnp.exp2(zn_sc[...] + lrn + between_n)
    for h in range(N_HEADS):
        sl = slice(h * HEAD_DIM, (h + 1) * HEAD_DIM)
        rows = slice(h * tn, (h + 1) * tn)
        a_new = jnp.where(col < row, an[rows, :], 0.0).astype(BF)
        o = _dot(ap[rows, :], vp_ref[0, :, sl].astype(BF)) + _dot(a_new, vn_ref[0, :, sl])
        o_ref[0, :, sl] = o.astype(BF)


def _sb_attention_past(q, kn, vn, cache_k, cache_v, layer):
    B, Tn, D = q.shape
    Tp = cache_k.shape[2]
    new = pl.BlockSpec((1, Tn, D), lambda b: (b, 0, 0))
    past = pl.BlockSpec((None, 1, Tp, D), lambda b: (layer, b, 0, 0))
    return pl.pallas_call(
        _sb_attn_past_kernel, grid=(B,),
        in_specs=[new, new, new, past, past], out_specs=new,
        scratch_shapes=[pltpu.VMEM((N_HEADS * Tn, Tp), F32), pltpu.VMEM((N_HEADS * Tn, Tp), F32),
                        pltpu.VMEM((N_HEADS * Tn, Tn), F32), pltpu.VMEM((N_HEADS * Tn, Tn), F32)],
        out_shape=jax.ShapeDtypeStruct((B, Tn, D), BF),
        compiler_params=_cparams("parallel"),
        name="sb_attention_past",
    )(q, kn, vn, cache_k, cache_v)


def _out_proj_kernel(a_ref, w_ref, x_ref, gate_ref, o_ref):
    o_ref[0] = x_ref[0] + gate_ref[0] * _dot(a_ref[0], w_ref[...])


def _out_proj(a, w, x, gate):
    B, T, D = x.shape
    K = a.shape[-1]
    tm = min(T, ROW_TILE)
    row = pl.BlockSpec((1, tm, D), lambda b, i: (b, i, 0))
    return pl.pallas_call(
        _out_proj_kernel, grid=(B, T // tm),
        in_specs=[pl.BlockSpec((1, tm, K), lambda b, i: (b, i, 0)), _const_spec((K, D)), row, _mod_spec(gate, tm)],
        out_specs=row, out_shape=jax.ShapeDtypeStruct((B, T, D), F32),
        compiler_params=_cparams("parallel", "parallel"),
        name="out_proj",
    )(a, w, x, gate)


def _top2_gates(logits):
    lane = lax.broadcasted_iota(jnp.int32, logits.shape, 1)
    logits = jnp.where(lane < N_EXPERTS, logits, -jnp.inf)
    v1 = jnp.max(logits, axis=-1, keepdims=True)
    i1 = jnp.min(jnp.where(logits == v1, lane, LANES), axis=-1, keepdims=True)
    rest = jnp.where(lane == i1, -jnp.inf, logits)
    v2 = jnp.max(rest, axis=-1, keepdims=True)
    i2 = jnp.min(jnp.where(rest == v2, lane, LANES), axis=-1, keepdims=True)
    e2 = jnp.exp(v2 - v1)
    return jnp.where(lane == i1, 1.0 / (1.0 + e2), 0.0) + jnp.where(lane == i2, e2 / (1.0 + e2), 0.0)


def _ffn_kernel(*refs, moe):
    if moe:
        (x_ref, g_ref, sh_ref, sc_ref, gate_ref, wr_ref, wg_ref, wu_ref, wd_ref,
         o_ref, h_sc, acc_sc, gates_sc) = refs
    else:
        x_ref, g_ref, sh_ref, sc_ref, gate_ref, wg_ref, wu_ref, wd_ref, o_ref, h_sc, acc_sc = refs
    f = pl.program_id(2)

    @pl.when(f == 0)
    def _():
        h = _normmod(x_ref[0], g_ref[...], sh_ref[0], sc_ref[0])
        h_sc[...] = h.astype(BF)
        acc_sc[...] = jnp.zeros_like(acc_sc)
        if moe:
            gates_sc[...] = _top2_gates(_dot(h, wr_ref[...], precision=HIGHEST))

    hb = h_sc[...]
    g = _dot(hb, wg_ref[...])
    u = _dot(hb, wu_ref[...])
    a = g * jax.nn.sigmoid(g) * u
    if moe:
        gates = gates_sc[...]
        lane = lax.broadcasted_iota(jnp.int32, gates.shape, 1)
        a = a * jnp.sum(jnp.where(lane == f, gates, 0.0), axis=-1, keepdims=True)
    acc_sc[...] += _dot(a.astype(BF), wd_ref[...])

    @pl.when(f == pl.num_programs(2) - 1)
    def _():
        o_ref[0] = x_ref[0] + gate_ref[0] * acc_sc[...]


def _ffn(x, gain, shift, scale, gate, w_gate, w_up, w_down, w_router=None):
    B, T, D = x.shape
    tm = min(T, FFN_ROW_TILE)
    moe = w_router is not None
    row = pl.BlockSpec((1, tm, D), lambda b, i, f: (b, i, 0))
    in_specs = [row, _const_spec((1, D)), _mod_spec(shift, tm), _mod_spec(scale, tm), _mod_spec(gate, tm)]
    args = [x, gain.reshape(1, D), shift, scale, gate]
    scratch = [pltpu.VMEM((tm, D), BF), pltpu.VMEM((tm, D), F32)]
    if moe:
        nf, _, tf = w_gate.shape
        in_specs += [_const_spec((D, LANES)),
                     pl.BlockSpec((None, D, tf), lambda b, i, f: (f, 0, 0)),
                     pl.BlockSpec((None, D, tf), lambda b, i, f: (f, 0, 0)),
                     pl.BlockSpec((None, tf, D), lambda b, i, f: (f, 0, 0))]
        args.append(jnp.pad(w_router, ((0, 0), (0, LANES - N_EXPERTS))))
        scratch.append(pltpu.VMEM((tm, LANES), F32))
    else:
        tf = FFN_COL_TILE
        nf = w_gate.shape[1] // tf
        in_specs += [pl.BlockSpec((D, tf), lambda b, i, f: (0, f)),
                     pl.BlockSpec((D, tf), lambda b, i, f: (0, f)),
                     pl.BlockSpec((tf, D), lambda b, i, f: (f, 0))]
    args += [w_gate, w_up, w_down]
    return pl.pallas_call(
        functools.partial(_ffn_kernel, moe=moe),
        grid=(B, T // tm, nf),
        in_specs=in_specs, out_specs=row, scratch_shapes=scratch,
        out_shape=jax.ShapeDtypeStruct((B, T, D), F32),
        compiler_params=_cparams("parallel", "parallel", "arbitrary"),
        name="moe_ffn" if moe else "dense_ffn",
    )(*args)


def _lru_proj_kernel(x_ref, g_ref, sh_ref, sc_ref, w_ref, gate_ref, xr_ref, tail_ref, *, tail_rows):
    h = _normmod(x_ref[0], g_ref[...], sh_ref[0], sc_ref[0])
    hb = h.astype(BF)
    tm = hb.shape[0]
    for c in range(2 * D_RNN // MXU_N):
        acc = _dot(hb, w_ref[:, c * MXU_N:(c + 1) * MXU_N])
        if c * MXU_N < D_RNN:
            gate_ref[0, :, c * MXU_N:(c + 1) * MXU_N] = jax.nn.gelu(acc).astype(BF)
        else:
            sl = slice(c * MXU_N - D_RNN, (c + 1) * MXU_N - D_RNN)
            xr_ref[0, :, sl] = acc.astype(BF)
            tail_ref[0, :, sl] = acc[tm - tail_rows:, :]


def _lru_proj(x, gain, shift, scale, w_in, all_rows_f32):
    B, T, D = x.shape
    tm = min(T, ROW_TILE)
    tail_rows = tm if all_rows_f32 else SUBLANES
    tail_spec = pl.BlockSpec((1, tail_rows, D_RNN), (lambda b, i: (b, i, 0)) if all_rows_f32
                                       else (lambda b, i: (b, 0, 0)))
    row = pl.BlockSpec((1, tm, D), lambda b, i: (b, i, 0))
    rnn = pl.BlockSpec((1, tm, D_RNN), lambda b, i: (b, i, 0))
    return pl.pallas_call(
        functools.partial(_lru_proj_kernel, tail_rows=tail_rows), grid=(B, T // tm),
        in_specs=[row, _const_spec((1, D)), _mod_spec(shift, tm), _mod_spec(scale, tm),
                  _const_spec((D, 2 * D_RNN))],
        out_specs=[rnn, rnn, tail_spec],
        out_shape=[jax.ShapeDtypeStruct((B, T, D_RNN), BF), jax.ShapeDtypeStruct((B, T, D_RNN), BF),
                   jax.ShapeDtypeStruct((B, T if all_rows_f32 else SUBLANES, D_RNN), F32)],
        compiler_params=_cparams("parallel", "arbitrary"),
        name="lru_proj",
    )(x, gain.reshape(1, D), shift, scale, w_in)


def _lru_core_kernel(gg_ref, xr_ref, x_ref, gate_ref, h0_ref, prev_ref, cw_ref, cb_ref, wri_ref, bri_ref,
                     lam_ref, wo_ref, o_ref, hl_ref, xp_sc, a_sc, b_sc, hs_sc, h_sc):
    i = pl.program_id(1)
    tm = xr_ref.shape[1]

    @pl.when(i == 0)
    def _():
        xp_sc[0:SUBLANES, :] = prev_ref[0]
        h_sc[...] = h0_ref[0]

    xp_sc[SUBLANES:, :] = xr_ref[0].astype(F32)
    xc = cb_ref[...]
    for j in range(CONV_W):
        off = SUBLANES - (CONV_W - 1) + j
        xc = xc + cw_ref[j:j + 1, :] * xp_sc[off:off + tm, :]
    xp_sc[0:SUBLANES, :] = xp_sc[tm:tm + SUBLANES, :]

    xcb = xc.astype(BF)
    neg_decay = -LRU_C * _softplus(-lam_ref[...])
    for s in range(D_RNN // RNN_SUPER):
        sl = slice(s * RNN_SUPER, (s + 1) * RNN_SUPER)
        z = _dot(xcb[:, sl], wri_ref[s])
        r = jax.nn.sigmoid(z[:, :RNN_SUPER] + bri_ref[0:1, sl])
        g = jax.nn.sigmoid(z[:, RNN_SUPER:] + bri_ref[1:2, sl])
        log_a = neg_decay[:, sl] * r
        y = 2.0 * log_a
        a_sc[:, sl] = jnp.exp(log_a)
        b_sc[:, sl] = jnp.sqrt(-jnp.tanh(log_a) * (jnp.exp(y) + 1.0)) * (g * xc[:, sl])

    def step(t, h):
        h = a_sc[pl.ds(t, 1), :] * h + b_sc[pl.ds(t, 1), :]
        hs_sc[pl.ds(t, 1), :] = h
        return h

    h_last = lax.fori_loop(0, tm, step, h_sc[...], unroll=8)
    h_sc[...] = h_last
    hl_ref[0] = h_last
    y = (gg_ref[0].astype(F32) * hs_sc[...]).astype(BF)
    o_ref[0] = x_ref[0] + gate_ref[0] * _dot(y, wo_ref[...])


def _block_diag(w):
    n_super = D_RNN // RNN_SUPER
    per = N_RNN_BLOCKS // n_super
    w = w.reshape(n_super, per, w.shape[1], w.shape[2])
    return jnp.einsum('sbij,bc->sbicj', w, jnp.eye(per, dtype=w.dtype)).reshape(n_super, RNN_SUPER, RNN_SUPER)


def _lru_core(gg, xr, x, gate, h0, prev, conv_w, conv_b, w_r, b_r, w_i, b_i, lam, w_out):
    B, T, D = x.shape
    tm = min(T, ROW_TILE)
    w_ri = jnp.concatenate([_block_diag(w_r), _block_diag(w_i)], axis=-1).astype(BF)
    b_ri = jnp.stack([b_r, b_i])
    row = pl.BlockSpec((1, tm, D), lambda b, i: (b, i, 0))
    rnn = pl.BlockSpec((1, tm, D_RNN), lambda b, i: (b, i, 0))
    n_super = D_RNN // RNN_SUPER
    return pl.pallas_call(
        _lru_core_kernel, grid=(B, T // tm),
        in_specs=[rnn, rnn, row, _mod_spec(gate, tm),
                  pl.BlockSpec((1, 1, D_RNN), lambda b, i: (b, 0, 0)),
                  pl.BlockSpec((1, SUBLANES, D_RNN), lambda b, i: (b, 0, 0)),
                  _const_spec((CONV_W, D_RNN)), _const_spec((1, D_RNN)),
                  _const_spec((n_super, RNN_SUPER, 2 * RNN_SUPER)), _const_spec((2, D_RNN)),
                  _const_spec((1, D_RNN)), _const_spec((D_RNN, D))],
        out_specs=[row, pl.BlockSpec((1, 1, D_RNN), lambda b, i: (b, 0, 0))],
        out_shape=[jax.ShapeDtypeStruct((B, T, D), F32), jax.ShapeDtypeStruct((B, 1, D_RNN), F32)],
        scratch_shapes=[pltpu.VMEM((tm + SUBLANES, D_RNN), F32), pltpu.VMEM((tm, D_RNN), F32),
                        pltpu.VMEM((tm, D_RNN), F32), pltpu.VMEM((tm, D_RNN), F32),
                        pltpu.VMEM((1, D_RNN), F32)],
        compiler_params=_cparams("parallel", "arbitrary"),
        name="lru_core",
    )(gg, xr, x, gate, h0, prev, conv_w, conv_b.reshape(1, D_RNN), w_ri, b_ri, lam.reshape(1, D_RNN), w_out)


def _trunk(x, mod, W, past):
    B, T, D = x.shape
    depth = mod.shape[0]
    has_past = past is not None
    flat = (lambda a: a.reshape(1, B * T, a.shape[-1])) if has_past else (lambda a: a)
    unflat = (lambda a: a.reshape(B, T, a.shape[-1])) if has_past else (lambda a: a)
    n_fox = len(range(0, depth, N_MIXERS))
    n_sb = len(range(2, depth, N_MIXERS))
    fox_kv = sb_kv = None
    fox_logf, lru_new = [], []
    for i in range(depth):
        per_batch = [m[:, None, :] for m in jnp.split(mod[i], 6, axis=-1)]
        if has_past:
            per_row = [jnp.broadcast_to(m, (B, T, D)).reshape(1, B * T, D) for m in per_batch]
        else:
            per_row = per_batch
        shift1, scale1, gate1, shift2, scale2, gate2 = per_row
        kind, j = i % N_MIXERS, i // N_MIXERS
        if kind == 0:
            fox_params = (W['fox_w_f'][j], W['fox_b_f'][j], W['fox_q_norm'][j], W['fox_k_norm'][j])
            q, k32, kbf, v32, vbf, logf = _qkv_proj(flat(x), W['norm_mix'][i], shift1, scale1,
                                                    W['fox_w_qkv'][j], fox_params, n_fox, j, fox_kv)
            fox_kv = (k32, v32)
            q, kbf, vbf, logf = (unflat(a) for a in (q, kbf, vbf, logf))
            fox_logf.append(logf[..., :N_HEADS])
            if has_past:
                cache_k, cache_v, cache_lf = past['fox']
                Tp = cache_k.shape[2]
                lf_all = jnp.concatenate([jnp.swapaxes(cache_lf[j], 1, 2),
                                          jnp.swapaxes(fox_logf[-1], 1, 2)], axis=-1)
                pad = (-lf_all.shape[-1]) % LANES
                cum = _cumsum_flat(jnp.pad(lf_all, ((0, 0), (0, 0), (0, pad))).reshape(B * N_HEADS, -1))
                cum = cum.reshape(B, N_HEADS, -1) * LOG2E
                o = _fox_attention_past(q, kbf, vbf, cache_k.reshape(*cache_k.shape[:3], D),
                                        cache_v.reshape(*cache_v.shape[:3], D), j,
                                        cum[..., :Tp], cum[..., Tp:Tp + T])
            else:
                bound = 1.02 * HEAD_DIM ** 0.5 * jnp.max(jnp.abs(W['fox_q_norm'][j])) \
                    * jnp.max(jnp.abs(W['fox_k_norm'][j]))
                o = _fox_attention(q, kbf, vbf, logf, bound, min(T, FOX_BLOCK))
            x = unflat(_out_proj(flat(o), W['fox_w_out'][j], flat(x), gate1))
        elif kind == 1:
            gg, xr, tail = _lru_proj(flat(x), W['norm_mix'][i], shift1, scale1, W['lru_w_in'][j], has_past)
            gg, xr, tail = unflat(gg), unflat(xr), unflat(tail) if has_past else tail
            if has_past:
                state_h, state_conv = past['lru']
                h0 = state_h[j][:, None, :]
                prev = jnp.pad(state_conv[j], ((0, 0), (SUBLANES - (CONV_W - 1), 0), (0, 0)))
            else:
                h0 = jnp.zeros((B, 1, D_RNN), F32)
                prev = jnp.zeros((B, SUBLANES, D_RNN), F32)
            x, h_last = _lru_core(gg, xr, x, per_batch[2], h0, prev, W['lru_conv_w'][j], W['lru_conv_b'][j],
                                  W['lru_w_r'][j], W['lru_b_r'][j], W['lru_w_i'][j], W['lru_b_i'][j],
                                  W['lru_lambda'][j], W['lru_w_out'][j])
            lru_new.append((h_last[:, 0], tail[:, tail.shape[1] - (CONV_W - 1):]))
        else:
            q, k32, kbf, v32, vbf = _qkv_proj(flat(x), W['norm_mix'][i], shift1, scale1, W['sb_w_in'][j],
                                              None, n_sb, j, sb_kv)
            sb_kv = (k32, v32)
            q, kbf, vbf = (unflat(a) for a in (q, kbf, vbf))
            if has_past:
                cache_k, cache_v = past['sb']
                o = _sb_attention_past(q, kbf, vbf, cache_k.reshape(*cache_k.shape[:3], D),
                                       cache_v.reshape(*cache_v.shape[:3], D), j)
            else:
                o = _sb_attention(q, kbf, vbf, min(T, SB_BLOCK))
            x = unflat(_out_proj(flat(o), W['sb_w_out'][j], flat(x), gate1))
        if i % 2 == 0:
            x = unflat(_ffn(flat(x), W['norm_ffn'][i], shift2, scale2, gate2,
                            W['ffn_w_gate'][i // 2], W['ffn_w_up'][i // 2], W['ffn_w_down'][i // 2]))
        else:
            x = unflat(_ffn(flat(x), W['norm_ffn'][i], shift2, scale2, gate2,
                            W['moe_w_gate'][i // 2], W['moe_w_up'][i // 2], W['moe_w_down'][i // 2],
                            W['moe_w_router'][i // 2]))

    def heads(state):
        return state.reshape(state.shape[0], B, T, N_HEADS, HEAD_DIM)

    fox_state = [heads(fox_kv[0]), heads(fox_kv[1]), jnp.stack(fox_logf)]
    lru_state = [jnp.stack([s[n] for s in lru_new]) for n in range(2)]
    sb_state = [heads(sb_kv[0]), heads(sb_kv[1])]
    return x, fox_state, lru_state, sb_state


MATMUL_WEIGHTS = ('fox_w_qkv', 'fox_w_out', 'lru_w_in', 'lru_w_out', 'sb_w_in', 'sb_w_out',
                  'ffn_w_gate', 'ffn_w_up', 'ffn_w_down', 'moe_w_gate', 'moe_w_up', 'moe_w_down')


def _prepare_weights(W):
    W = dict(W)
    w_in = W.pop('fox_w_in')
    W['fox_w_qkv'] = w_in[:, :, :3 * D_MODEL]
    W['fox_w_f'] = w_in[:, :, 3 * D_MODEL:]
    for name in MATMUL_WEIGHTS:
        W[name] = W[name].astype(BF)
    return W


def _forward(W, x_prompt, c_prompt, x_sample, c_sample, past):
    W = _prepare_weights(W)
    n_prompt, n_sample = c_prompt.shape[0], c_sample.shape[0]
    c_all = jnp.concatenate([c_prompt, c_sample], axis=0)
    c_all = jnp.pad(c_all, ((0, (-c_all.shape[0]) % (2 * SUBLANES)), (0, 0)))
    mod = _ada(c_all, W['w_ada'], W['b_ada'])
    y_p, fox_p, lru_p, sb_p = _trunk(x_prompt, mod[:, :n_prompt], W, None)
    y_s, fox_s, lru_s, sb_s = _trunk(x_sample, mod[:, n_prompt:n_prompt + n_sample], W, past)

    return (y_p, y_s, *fox_p, *fox_s, *lru_p, *lru_s, *sb_p, *sb_s)


def kernel(x_prompt, x_sample, cache_fox_k, cache_fox_v, cache_fox_logf, state_lru_h, state_lru_conv,
           cache_sb_k, cache_sb_v, c_prompt, c_sample, norm_mix, norm_ffn, w_ada, b_ada,
           fox_w_in, fox_b_f, fox_q_norm, fox_k_norm, fox_w_out,
           lru_w_in, lru_conv_w, lru_conv_b, lru_w_r, lru_b_r, lru_w_i, lru_b_i, lru_lambda, lru_w_out,
           sb_w_in, sb_w_out, ffn_w_gate, ffn_w_up, ffn_w_down,
           moe_w_router, moe_w_gate, moe_w_up, moe_w_down):
    W = {
        'norm_mix': norm_mix, 'norm_ffn': norm_ffn, 'w_ada': w_ada, 'b_ada': b_ada,
        'fox_w_in': fox_w_in, 'fox_b_f': fox_b_f, 'fox_q_norm': fox_q_norm,
        'fox_k_norm': fox_k_norm, 'fox_w_out': fox_w_out,
        'lru_w_in': lru_w_in, 'lru_conv_w': lru_conv_w, 'lru_conv_b': lru_conv_b,
        'lru_w_r': lru_w_r, 'lru_b_r': lru_b_r, 'lru_w_i': lru_w_i, 'lru_b_i': lru_b_i,
        'lru_lambda': lru_lambda, 'lru_w_out': lru_w_out,
        'sb_w_in': sb_w_in, 'sb_w_out': sb_w_out,
        'ffn_w_gate': ffn_w_gate, 'ffn_w_up': ffn_w_up, 'ffn_w_down': ffn_w_down,
        'moe_w_router': moe_w_router, 'moe_w_gate': moe_w_gate, 'moe_w_up': moe_w_up,
        'moe_w_down': moe_w_down,
    }
    past = {'fox': (cache_fox_k, cache_fox_v, cache_fox_logf), 'lru': (state_lru_h, state_lru_conv),
            'sb': (cache_sb_k, cache_sb_v)}
    return _forward(W, x_prompt, c_prompt, x_sample, c_sample, past)
```

```python
import functools

import jax
import jax.numpy as jnp
from jax import lax
from jax.experimental import pallas as pl
from jax.experimental.pallas import tpu as pltpu

F32 = jnp.float32
BF = jnp.bfloat16
HIGHEST = lax.Precision.HIGHEST

D_MODEL = 1024
N_HEADS = 8
HEAD_DIM = 128
N_MIXERS = 3
D_RNN = 1280
N_RNN_BLOCKS = 16
RNN_SUPER = 640
CONV_W = 4
LRU_C = 8.0
N_EXPERTS = 8
EPS = 1e-6
ATTN_SCALE = HEAD_DIM ** -0.5
LANES = 128
SUBLANES = 8
MXU_N = 256
EXP_UNDERFLOW = 104.0
EXP2_UNDERFLOW = 150.0
LOG2E = 1.4426950408889634
Q_SCALE = ATTN_SCALE * LOG2E
EXP2_FLUSH = 126.0
FIXED_REF_MAX_BOUND = 24.0
VMEM_LIMIT = 56 * 2 ** 20

FOX_BLOCK = 512
SB_BLOCK = 256
ROW_TILE = 512
FFN_ROW_TILE = 1024
FFN_COL_TILE = 512


def _cparams(*sem):
    return pltpu.CompilerParams(dimension_semantics=sem, vmem_limit_bytes=VMEM_LIMIT)


def _dot(a, b, **kw):
    return jnp.dot(a, b, preferred_element_type=F32, **kw)


def _dot_nt(a, b):
    return lax.dot_general(a, b, (((1,), (1,)), ((), ())), preferred_element_type=F32)


def _normmod(x, gain, shift, scale):
    ms = jnp.mean(x * x, axis=-1, keepdims=True)
    return x * lax.rsqrt(ms + EPS) * gain * (1.0 + scale) + shift


def _softplus(x):
    return jnp.maximum(x, 0.0) + jnp.log1p(jnp.exp(-jnp.abs(x)))


def _mod_spec(mod, tm):
    d = mod.shape[-1]
    if mod.shape[1] == 1:
        return pl.BlockSpec((1, 1, d), lambda b, i, *_: (b, 0, 0))
    return pl.BlockSpec((1, tm, d), lambda b, i, *_: (b, i, 0))


def _const_spec(shape):
    zeros = (0,) * len(shape)
    return pl.BlockSpec(shape, lambda *_: zeros)


def _ada_kernel(c_ref, w_ref, b_ref, o_ref):
    c = c_ref[...]
    c_act = (c * jax.nn.sigmoid(c)).astype(BF)
    o_ref[0] = _dot(c_act, w_ref[0].astype(BF)) + b_ref[0]


def _ada(c_all, w_ada, b_ada):
    rows, d = c_all.shape
    depth, _, n = w_ada.shape
    tn = 1024
    return pl.pallas_call(
        _ada_kernel,
        grid=(depth, n // tn),
        in_specs=[pl.BlockSpec((rows, d), lambda l, j: (0, 0)),
                  pl.BlockSpec((1, d, tn), lambda l, j: (l, 0, j)),
                  pl.BlockSpec((1, 1, tn), lambda l, j: (l, 0, j))],
        out_specs=pl.BlockSpec((1, rows, tn), lambda l, j: (l, 0, j)),
        out_shape=jax.ShapeDtypeStruct((depth, rows, n), F32),
        compiler_params=_cparams("parallel", "parallel"),
        name="ada_modulation",
    )(c_all, w_ada, b_ada.reshape(depth, 1, n))


def _qkv_kernel(*refs, fox, aliased):
    refs = list(refs)
    x_ref, g_ref, sh_ref, sc_ref, w_ref = refs[:5]
    del refs[:5]
    if fox:
        wf_ref, bf_ref, qg_ref, kg_ref = refs[:4]
        del refs[:4]
    if aliased:
        del refs[:2]
    q_ref, k32_ref, kbf_ref, v32_ref, vbf_ref = refs[:5]
    h = _normmod(x_ref[0], g_ref[...], sh_ref[0], sc_ref[0])
    hb = h.astype(BF)
    tm = hb.shape[0]

    def head_norm(a, gain_ref):
        ms = jnp.mean(a * a, axis=-1, keepdims=True)
        return a * lax.rsqrt(ms + EPS) * gain_ref[...]

    for c in range(3 * D_MODEL // MXU_N):
        acc = _dot(hb, w_ref[:, c * MXU_N:(c + 1) * MXU_N])
        part = (c * MXU_N) // D_MODEL
        for hh in range(MXU_N // HEAD_DIM):
            col = (c * MXU_N) % D_MODEL + hh * HEAD_DIM
            sl = slice(col, col + HEAD_DIM)
            state_rows = pl.ds(col // HEAD_DIM, tm, stride=N_HEADS)
            a = acc[:, hh * HEAD_DIM:(hh + 1) * HEAD_DIM]
            if part == 0:
                if fox:
                    a = head_norm(a, qg_ref)
                q_ref[0, :, sl] = (a * Q_SCALE).astype(BF)
            elif part == 1:
                if fox:
                    a = head_norm(a, kg_ref)
                k32_ref[0, state_rows, :] = a
                kbf_ref[0, :, sl] = a.astype(BF)
            else:
                v32_ref[0, state_rows, :] = a
                vbf_ref[0, :, sl] = a.astype(BF)
    if fox:
        f_logit = _dot(h, wf_ref[...], precision=HIGHEST) + bf_ref[...]
        refs[5][0] = -_softplus(-f_logit)


def _qkv_proj(x, gain, shift, scale, w_qkv, fox_params=None, n_slots=1, slot=0, state=None):
    B, T, D = x.shape
    tm = min(T, ROW_TILE)
    fox = fox_params is not None
    aliased = state is not None
    row = pl.BlockSpec((1, tm, D), lambda b, i: (b, i, 0))
    state_spec = pl.BlockSpec((None, 1, tm * N_HEADS, HEAD_DIM), lambda b, i: (slot, b, i, 0))
    state_shape = jax.ShapeDtypeStruct((n_slots, B, T * N_HEADS, HEAD_DIM), F32)
    in_specs = [row, _const_spec((1, D)), _mod_spec(shift, tm), _mod_spec(scale, tm), _const_spec((D, 3 * D))]
    args = [x, gain.reshape(1, D), shift, scale, w_qkv]
    out_specs = [row, state_spec, row, state_spec, row]
    out_shape = [jax.ShapeDtypeStruct((B, T, D), BF), state_shape, jax.ShapeDtypeStruct((B, T, D), BF),
                 state_shape, jax.ShapeDtypeStruct((B, T, D), BF)]
    if fox:
        w_f, b_f, q_gain, k_gain = fox_params
        in_specs += [_const_spec((D, LANES)), _const_spec((1, LANES)),
                     _const_spec((1, HEAD_DIM)), _const_spec((1, HEAD_DIM))]
        args += [jnp.pad(w_f, ((0, 0), (0, LANES - N_HEADS))),
                 jnp.pad(b_f, (0, LANES - N_HEADS)).reshape(1, LANES),
                 q_gain.reshape(1, HEAD_DIM), k_gain.reshape(1, HEAD_DIM)]
        out_specs.append(pl.BlockSpec((1, tm, LANES), lambda b, i: (b, i, 0)))
        out_shape.append(jax.ShapeDtypeStruct((B, T, LANES), F32))
    aliases = {}
    if aliased:
        aliases = {len(args): 1, len(args) + 1: 3}
        in_specs += [pl.BlockSpec(memory_space=pl.ANY)] * 2
        args += list(state)
    return pl.pallas_call(
        functools.partial(_qkv_kernel, fox=fox, aliased=aliased),
        grid=(B, T // tm),
        in_specs=in_specs, out_specs=out_specs, out_shape=out_shape,
        input_output_aliases=aliases,
        compiler_params=_cparams("parallel", "parallel"),
        name="fox_qkv_proj" if fox else "sb_qkv_proj",
    )(*args)


def _split3_bf16(x):
    hi = x.astype(BF).astype(F32)
    mid = (x - hi).astype(BF).astype(F32)
    return hi, mid, (x - hi - mid).astype(BF).astype(F32)


def _fox_prep_kernel(bound_ref, lf_ref, qx_ref, kx_ref, cum_ref, bmin_ref, bmax_ref, carry_sc):
    i = pl.program_id(1)
    tm = lf_ref.shape[1]

    @pl.when(i == 0)
    def _():
        carry_sc[...] = jnp.zeros_like(carry_sc)

    r = lax.broadcasted_iota(jnp.int32, (tm, tm), 0)
    c = lax.broadcasted_iota(jnp.int32, (tm, tm), 1)
    cum = _dot((c <= r).astype(F32), lf_ref[0], precision=HIGHEST) + carry_sc[...]
    carry_sc[...] = cum[tm - 1:tm, :]
    cum_ref[0] = cum
    bmin_ref[0, 0] = jnp.min(cum, axis=0, keepdims=True)
    bmax_ref[0, 0] = jnp.max(cum, axis=0, keepdims=True)
    c2 = cum * LOG2E
    bound2 = bound_ref[0] * LOG2E
    lane = lax.broadcasted_iota(jnp.int32, (tm, LANES), 1)
    for h in range(N_HEADS):
        col = c2[:, h:h + 1]
        k1, k2, k3 = _split3_bf16(-col)
        q1, q2, q3 = _split3_bf16(col - bound2)
        kx = jnp.where(lane == 0, k1, jnp.where(lane == 1, k2, jnp.where(lane == 2, k3,
                                                                         jnp.where(lane < 6, 1.0, 0.0))))
        qx = jnp.where(lane < 3, 1.0, jnp.where(lane == 3, q1, jnp.where(lane == 4, q2,
                                                                         jnp.where(lane == 5, q3, 0.0))))
        sl = slice(h * LANES, (h + 1) * LANES)
        qx_ref[0, :, sl] = qx.astype(BF)
        kx_ref[0, :, sl] = kx.astype(BF)


def _fox_prep(logf, bound, blk):
    B, T, _ = logf.shape
    nb = T // blk
    row = pl.BlockSpec((1, blk, LANES), lambda b, i: (b, i, 0))
    wide = pl.BlockSpec((1, blk, N_HEADS * LANES), lambda b, i: (b, i, 0))
    stat = pl.BlockSpec((1, 1, 1, LANES), lambda b, i: (b, i, 0, 0))
    return pl.pallas_call(
        _fox_prep_kernel, grid=(B, nb),
        in_specs=[pl.BlockSpec(memory_space=pltpu.SMEM), row],
        out_specs=[wide, wide, row, stat, stat],
        out_shape=[jax.ShapeDtypeStruct((B, T, N_HEADS * LANES), BF)] * 2
        + [jax.ShapeDtypeStruct((B, T, LANES), F32)] + [jax.ShapeDtypeStruct((B, nb, 1, LANES), F32)] * 2,
        scratch_shapes=[pltpu.VMEM((1, LANES), F32)],
        compiler_params=_cparams("parallel", "arbitrary"),
        name="fox_prep",
    )(bound.reshape(1), logf)


def _cumsum_flat_kernel(x_ref, o_ref):
    n = x_ref.shape[-1]
    r = lax.broadcasted_iota(jnp.int32, (n, n), 0)
    c = lax.broadcasted_iota(jnp.int32, (n, n), 1)
    o_ref[...] = _dot(x_ref[...], (r <= c).astype(F32), precision=HIGHEST)


def _cumsum_flat(x):
    return pl.pallas_call(
        _cumsum_flat_kernel, out_shape=jax.ShapeDtypeStruct(x.shape, F32),
        compiler_params=pltpu.CompilerParams(vmem_limit_bytes=VMEM_LIMIT),
        name="cumsum_flat",
    )(x)


def _fox_attn_kernel(js_ref, q_ref, qx_ref, k_ref, kx_ref, v_ref, o_ref, acc_sc, *, blk):
    b, h, i = pl.program_id(0), pl.program_id(1), pl.program_id(2)
    nq = pl.num_programs(2)
    q = jnp.concatenate([q_ref[0], qx_ref[0]], axis=-1)
    acc_sc[...] = jnp.zeros_like(acc_sc)
    lane = lax.broadcasted_iota(jnp.int32, (blk, HEAD_DIM), 1)
    ones_col = (lane == 0).astype(BF)

    def block(j, diagonal):
        start = pl.multiple_of(j * blk, blk)
        kb = jnp.concatenate([k_ref[0, pl.ds(start, blk), :], kx_ref[0, pl.ds(start, blk), :]], axis=-1)
        vb = jnp.concatenate([v_ref[0, pl.ds(start, blk), :], ones_col], axis=-1)
        s = _dot_nt(q, kb)
        if diagonal:
            row = lax.broadcasted_iota(jnp.int32, (blk, blk), 0)
            col = lax.broadcasted_iota(jnp.int32, (blk, blk), 1)
            s = jnp.where(col <= row, s, -jnp.inf)
        acc_sc[...] += _dot(jnp.exp2(s).astype(BF), vb)

    j0 = js_ref[(b * N_HEADS + h) * nq + i]
    n_before = i - j0

    def pair(t, carry):
        block(j0 + 2 * t, False)
        block(j0 + 2 * t + 1, False)
        return carry

    lax.fori_loop(0, n_before // 2, pair, 0)

    @pl.when(n_before % 2 == 1)
    def _():
        block(i - 1, False)
        block(i, True)

    @pl.when(n_before % 2 == 0)
    def _():
        block(i, True)

    acc = acc_sc[...]
    o_ref[0] = (acc[:, :HEAD_DIM] / acc[:, HEAD_DIM:HEAD_DIM + 1]).astype(BF)


def _fox_attn_online_kernel(js_ref, q_ref, k_ref, v_ref, ck_ref, o_ref, m_sc, l_sc, acc_sc, *, blk):
    b, h, i = pl.program_id(0), pl.program_id(1), pl.program_id(2)
    nq = pl.num_programs(2)
    q = q_ref[0]
    m_sc[...] = jnp.full_like(m_sc, -jnp.inf)
    l_sc[...] = jnp.zeros_like(l_sc)
    acc_sc[...] = jnp.zeros_like(acc_sc)

    def block(j, diagonal):
        start = pl.multiple_of(j * blk, blk)
        kb = k_ref[0, pl.ds(start, blk), :]
        vb = v_ref[0, pl.ds(start, blk), :]
        s = _dot_nt(q, kb) - ck_ref[0, j]
        if diagonal:
            row = lax.broadcasted_iota(jnp.int32, (blk, blk), 0)
            col = lax.broadcasted_iota(jnp.int32, (blk, blk), 1)
            s = jnp.where(col <= row, s, -jnp.inf)
        m_prev = m_sc[...]
        m_new = jnp.maximum(m_prev, jnp.max(s, axis=-1, keepdims=True))
        alpha = jnp.exp2(m_prev - m_new)
        p = jnp.exp2(s - m_new)
        l_sc[...] = alpha * l_sc[...] + jnp.sum(p, axis=-1, keepdims=True)
        acc_sc[...] = alpha * acc_sc[...] + _dot(p.astype(BF), vb)
        m_sc[...] = m_new

    def body(j, carry):
        block(j, False)
        return carry

    lax.fori_loop(js_ref[(b * N_HEADS + h) * nq + i], i, body, 0)
    block(i, True)
    o_ref[0] = (acc_sc[...] / l_sc[...]).astype(BF)


def _fox_first_blocks(block_min, block_max, dead_gap):
    nb = block_min.shape[-1]
    j = jnp.arange(nb)
    upto = j[None, :] <= j[:, None]
    prefix_min = jnp.min(jnp.where(upto, block_min[..., None, :], jnp.inf), axis=-1)
    dead = (prefix_min[..., None, :] - block_max[..., :, None]) > dead_gap
    return jnp.sum(dead & (j[None, :] < j[:, None]), axis=-1).astype(jnp.int32)


def _fox_attention(q, kbf, vbf, logf, logit_bound, blk):
    B, T, D = q.shape
    nq = T // blk
    qx, kx, cum, bmin, bmax = _fox_prep(logf, logit_bound, blk)
    heads_first = lambda a: jnp.swapaxes(a[:, :, 0, :N_HEADS], 1, 2)
    use_fixed_reference = logit_bound < FIXED_REF_MAX_BOUND
    dead_gap = jnp.where(use_fixed_reference, (EXP2_FLUSH + 1.0) / LOG2E,
                         2.0 * logit_bound + EXP_UNDERFLOW + 1.0)
    first_blocks = _fox_first_blocks(heads_first(bmin), heads_first(bmax), dead_gap).reshape(-1)
    qspec = pl.BlockSpec((1, blk, HEAD_DIM), lambda b, h, i, js: (b, i, h))
    kvspec = pl.BlockSpec((1, T, HEAD_DIM), lambda b, h, i, js: (b, 0, h))
    out_shape = jax.ShapeDtypeStruct((B, T, D), BF)

    def fixed_reference():
        return pl.pallas_call(
            functools.partial(_fox_attn_kernel, blk=blk),
            grid_spec=pltpu.PrefetchScalarGridSpec(
                num_scalar_prefetch=1, grid=(B, N_HEADS, nq),
                in_specs=[qspec, qspec, kvspec, kvspec, kvspec], out_specs=qspec,
                scratch_shapes=[pltpu.VMEM((blk, 2 * HEAD_DIM), F32)]),
            out_shape=out_shape,
            compiler_params=_cparams("parallel", "parallel", "arbitrary"),
            name="fox_attention",
        )(first_blocks, q, qx, kbf, kx, vbf)

    def online():
        ck = jnp.swapaxes(cum[:, :, :N_HEADS], 1, 2) * LOG2E
        ckspec = pl.BlockSpec((1, nq, 1, blk), lambda b, h, i, js: (b * N_HEADS + h, 0, 0, 0))
        return pl.pallas_call(
            functools.partial(_fox_attn_online_kernel, blk=blk),
            grid_spec=pltpu.PrefetchScalarGridSpec(
                num_scalar_prefetch=1, grid=(B, N_HEADS, nq),
                in_specs=[qspec, kvspec, kvspec, ckspec], out_specs=qspec,
                scratch_shapes=[pltpu.VMEM((blk, 1), F32), pltpu.VMEM((blk, 1), F32),
                                pltpu.VMEM((blk, HEAD_DIM), F32)]),
            out_shape=out_shape,
            compiler_params=_cparams("parallel", "parallel", "arbitrary"),
            name="fox_attention_online",
        )(first_blocks, q, kbf, vbf, ck.reshape(B * N_HEADS, nq, 1, blk))

    return lax.cond(use_fixed_reference, fixed_reference, online)


def _fox_attn_past_kernel(q_ref, kn_ref, vn_ref, kp_ref, vp_ref, cp_ref, cn_ref, o_ref):
    tn = q_ref.shape[1]
    row = lax.broadcasted_iota(jnp.int32, (tn, tn), 0)
    col = lax.broadcasted_iota(jnp.int32, (tn, tn), 1)
    for h in range(N_HEADS):
        sl = slice(h * HEAD_DIM, (h + 1) * HEAD_DIM)
        q = q_ref[0, :, sl]
        sp = _dot_nt(q, kp_ref[0, :, sl].astype(BF)) - cp_ref[0, h:h + 1, :]
        sn = _dot_nt(q, kn_ref[0, :, sl]) - cn_ref[0, h:h + 1, :]
        sn = jnp.where(col <= row, sn, -jnp.inf)
        m = jnp.maximum(jnp.max(sp, axis=-1, keepdims=True), jnp.max(sn, axis=-1, keepdims=True))
        pp = jnp.exp2(sp - m)
        pn = jnp.exp2(sn - m)
        l = jnp.sum(pp, axis=-1, keepdims=True) + jnp.sum(pn, axis=-1, keepdims=True)
        o = _dot(pp.astype(BF), vp_ref[0, :, sl].astype(BF)) + _dot(pn.astype(BF), vn_ref[0, :, sl])
        o_ref[0, :, sl] = (o / l).astype(BF)


def _fox_attention_past(q, kn, vn, cache_k, cache_v, layer, cum_past, cum_new):
    B, Tn, D = q.shape
    Tp = cache_k.shape[2]
    new = pl.BlockSpec((1, Tn, D), lambda b: (b, 0, 0))
    past = pl.BlockSpec((None, 1, Tp, D), lambda b: (layer, b, 0, 0))
    return pl.pallas_call(
        _fox_attn_past_kernel, grid=(B,),
        in_specs=[new, new, new, past, past,
                  pl.BlockSpec((1, N_HEADS, Tp), lambda b: (b, 0, 0)),
                  pl.BlockSpec((1, N_HEADS, Tn), lambda b: (b, 0, 0))],
        out_specs=new, out_shape=jax.ShapeDtypeStruct((B, Tn, D), BF),
        compiler_params=_cparams("parallel"),
        name="fox_attention_past",
    )(q, kn, vn, cache_k, cache_v, cum_past, cum_new)


def _split_bf16(x):
    hi = x.astype(BF)
    return hi, (x - hi.astype(F32)).astype(BF)


def _neg_log2_rest(z2):
    zc = jnp.minimum(z2, 30.0)
    return jnp.log(1.0 + jnp.exp2(zc)) * LOG2E + (z2 - zc)


class _SbBlock:
    def __init__(self, blk):
        self.row = lax.broadcasted_iota(jnp.int32, (blk, blk), 0)
        self.col = lax.broadcasted_iota(jnp.int32, (blk, blk), 1)
        suffix = jnp.where(self.row >= self.col, -1.0, 0.0).astype(BF)
        self.suffix2 = jnp.concatenate([suffix, suffix], axis=0)

    def scores(self, q, kb, diagonal):
        z2 = _dot_nt(q, kb)
        nlr = _neg_log2_rest(z2)
        if diagonal:
            nlr = jnp.where(self.col < self.row, nlr, 0.0)
        hi, lo = _split_bf16(nlr)
        tail = _dot(jnp.concatenate([hi, lo], axis=-1), self.suffix2)
        return z2 + tail, tail[:, 0:1]

    def weighted(self, e, later, vb, diagonal):
        a = jnp.exp2(e + later)
        if diagonal:
            a = jnp.where(self.col < self.row, a, 0.0)
        return _dot(a.astype(BF), vb)


def _sb_attn_window_kernel(q_ref, kd_ref, kp_ref, vd_ref, vp_ref, o_ref, live_ref, *, blk):
    i = pl.program_id(1)
    tile = _SbBlock(blk)
    lane = lax.broadcasted_iota(jnp.int32, (1, LANES), 1)

    def heads(with_previous):
        live = jnp.full((1, LANES), -jnp.inf, F32)
        for h in range(N_HEADS):
            sl = slice(h * HEAD_DIM, (h + 1) * HEAD_DIM)
            q = q_ref[0, :, sl]
            e_d, later = tile.scores(q, kd_ref[0, :, sl], True)
            o = tile.weighted(e_d, 0.0, vd_ref[0, :, sl], True)
            if with_previous:
                e_p, total_p = tile.scores(q, kp_ref[0, :, sl], False)
                o = o + tile.weighted(e_p, later, vp_ref[0, :, sl], False)
                later = later + total_p
            o_ref[0, :, sl] = o.astype(BF)
            live = jnp.where(lane == h, jnp.max(later, axis=0, keepdims=True), live)
        live_ref[0, 0] = live

    @pl.when(i == 0)
    def _():
        heads(False)

    @pl.when(i > 0)
    def _():
        heads(True)


def _sb_attn_kernel(q_ref, k_ref, v_ref, o_ref, later_sc, acc_sc, *, blk):
    i = pl.program_id(2)
    q = q_ref[0]
    tile = _SbBlock(blk)

    def kv(ref, j):
        return ref[0, pl.ds(pl.multiple_of(j * blk, blk), blk), :]

    e, total = tile.scores(q, kv(k_ref, i), True)
    acc_sc[...] = tile.weighted(e, 0.0, kv(v_ref, i), True)
    later_sc[...] = total

    def cond(carry):
        j, live = carry
        return jnp.logical_and(j >= 0, live > -EXP2_UNDERFLOW)

    def body(carry):
        j, _ = carry
        e, total = tile.scores(q, kv(k_ref, j), False)
        later = later_sc[...]
        acc_sc[...] += tile.weighted(e, later, kv(v_ref, j), False)
        later_sc[...] = later + total
        return j - 1, jnp.max(later_sc[...])

    lax.while_loop(cond, body, (i - 1, jnp.max(later_sc[...])))
    o_ref[0] = acc_sc[...].astype(BF)


def _sb_attention(q, kbf, vbf, blk):
    B, T, D = q.shape
    nq = T // blk
    own = pl.BlockSpec((1, blk, D), lambda b, i: (b, i, 0))
    before = pl.BlockSpec((1, blk, D), lambda b, i: (b, jnp.maximum(i - 1, 0), 0))
    o_window, live = pl.pallas_call(
        functools.partial(_sb_attn_window_kernel, blk=blk),
        grid=(B, nq),
        in_specs=[own, own, before, own, before],
        out_specs=[own, pl.BlockSpec((1, 1, 1, LANES), lambda b, i: (b, i, 0, 0))],
        out_shape=[jax.ShapeDtypeStruct((B, T, D), BF), jax.ShapeDtypeStruct((B, nq, 1, LANES), F32)],
        compiler_params=_cparams("parallel", "parallel"),
        name="sb_attention_window",
    )(q, kbf, kbf, vbf, vbf)

    def walk():
        qspec = pl.BlockSpec((1, blk, HEAD_DIM), lambda b, h, i: (b, i, h))
        kvspec = pl.BlockSpec((1, T, HEAD_DIM), lambda b, h, i: (b, 0, h))
        return pl.pallas_call(
            functools.partial(_sb_attn_kernel, blk=blk),
            grid=(B, N_HEADS, nq),
            in_specs=[qspec, kvspec, kvspec], out_specs=qspec,
            scratch_shapes=[pltpu.VMEM((blk, 1), F32), pltpu.VMEM((blk, HEAD_DIM), F32)],
            out_shape=jax.ShapeDtypeStruct((B, T, D), BF),
            compiler_params=_cparams("parallel", "parallel", "arbitrary"),
            name="sb_attention_walk",
        )(q, kbf, vbf)

    unfinished = jnp.any(live[:, 2:, 0, :N_HEADS] > -EXP2_UNDERFLOW)
    return lax.cond(unfinished, walk, lambda: o_window)


def _sb_attn_past_kernel(q_ref, kn_ref, vn_ref, kp_ref, vp_ref, o_ref, zp_sc, lrp_sc, zn_sc, lrn_sc):
    tn = q_ref.shape[1]
    tp = kp_ref.shape[1]
    row = lax.broadcasted_iota(jnp.int32, (tn, tn), 0)
    col = lax.broadcasted_iota(jnp.int32, (tn, tn), 1)
    for h in range(N_HEADS):
        sl = slice(h * HEAD_DIM, (h + 1) * HEAD_DIM)
        rows = slice(h * tn, (h + 1) * tn)
        q = q_ref[0, :, sl]
        zp = _dot_nt(q, kp_ref[0, :, sl].astype(BF))
        zn = _dot_nt(q, kn_ref[0, :, sl])
        zp_sc[rows, :] = zp
        lrp_sc[rows, :] = -_neg_log2_rest(zp)
        zn_sc[rows, :] = zn
        lrn_sc[rows, :] = jnp.where(col < row, -_neg_log2_rest(zn), 0.0)
    rp = lax.broadcasted_iota(jnp.int32, (tp, tp), 0)
    cp = lax.broadcasted_iota(jnp.int32, (tp, tp), 1)
    lrp = lrp_sc[...]
    lrn = lrn_sc[...]
    hi, lo = _split_bf16(lrp)
    later_p = (rp > cp).astype(BF)
    between_p = _dot(hi, later_p) + _dot(lo, later_p) + jnp.sum(lrn, axis=-1, keepdims=True)
    hi, lo = _split_bf16(lrn)
    later_n = (row > col).astype(BF)
    between_n = _dot(hi, later_n) + _dot(lo, later_n)
    ap = jnp.exp2(zp_sc[...] + lrp + between_p).astype(BF)
    an = jnp.exp2(zn_sc[...] + lrn + between_n)
    for h in range(N_HEADS):
        sl = slice(h * HEAD_DIM, (h + 1) * HEAD_DIM)
        rows = slice(h * tn, (h + 1) * tn)
        a_new = jnp.where(col < row, an[rows, :], 0.0).astype(BF)
        o = _dot(ap[rows, :], vp_ref[0, :, sl].astype(BF)) + _dot(a_new, vn_ref[0, :, sl])
        o_ref[0, :, sl] = o.astype(BF)


def _sb_attention_past(q, kn, vn, cache_k, cache_v, layer):
    B, Tn, D = q.shape
    Tp = cache_k.shape[2]
    new = pl.BlockSpec((1, Tn, D), lambda b: (b, 0, 0))
    past = pl.BlockSpec((None, 1, Tp, D), lambda b: (layer, b, 0, 0))
    return pl.pallas_call(
        _sb_attn_past_kernel, grid=(B,),
        in_specs=[new, new, new, past, past], out_specs=new,
        scratch_shapes=[pltpu.VMEM((N_HEADS * Tn, Tp), F32), pltpu.VMEM((N_HEADS * Tn, Tp), F32),
                        pltpu.VMEM((N_HEADS * Tn, Tn), F32), pltpu.VMEM((N_HEADS * Tn, Tn), F32)],
        out_shape=jax.ShapeDtypeStruct((B, Tn, D), BF),
        compiler_params=_cparams("parallel"),
        name="sb_attention_past",
    )(q, kn, vn, cache_k, cache_v)


def _top2_gates(logits):
    lane = lax.broadcasted_iota(jnp.int32, logits.shape, 1)
    logits = jnp.where(lane < N_EXPERTS, logits, -jnp.inf)
    v1 = jnp.max(logits, axis=-1, keepdims=True)
    i1 = jnp.min(jnp.where(logits == v1, lane, LANES), axis=-1, keepdims=True)
    rest = jnp.where(lane == i1, -jnp.inf, logits)
    v2 = jnp.max(rest, axis=-1, keepdims=True)
    i2 = jnp.min(jnp.where(rest == v2, lane, LANES), axis=-1, keepdims=True)
    e2 = jnp.exp(v2 - v1)
    return jnp.where(lane == i1, 1.0 / (1.0 + e2), 0.0) + jnp.where(lane == i2, e2 / (1.0 + e2), 0.0)


def _ffn_kernel(*refs, moe, mixed):
    refs = list(refs)
    x_ref, g_ref, sh_ref, sc_ref, gate_ref = refs[:5]
    del refs[:5]
    if mixed:
        a_ref, wo_ref, mix_gate_ref = refs[:3]
        del refs[:3]
    if moe:
        wr_ref = refs.pop(0)
    wg_ref, wu_ref, wd_ref, o_ref, h_sc, acc_sc = refs[:6]
    f = pl.program_id(2)

    @pl.when(f == 0)
    def _():
        x = x_ref[0]
        if mixed:
            x = x + mix_gate_ref[0] * _dot(a_ref[0], wo_ref[...])
        o_ref[0] = x
        h = _normmod(x, g_ref[...], sh_ref[0], sc_ref[0])
        h_sc[...] = h.astype(BF)
        acc_sc[...] = jnp.zeros_like(acc_sc)
        if moe:
            refs[6][...] = _top2_gates(_dot(h, wr_ref[...], precision=HIGHEST))

    hb = h_sc[...]
    g = _dot(hb, wg_ref[...])
    u = _dot(hb, wu_ref[...])
    a = g * jax.nn.sigmoid(g) * u
    if moe:
        gates = refs[6][...]
        lane = lax.broadcasted_iota(jnp.int32, gates.shape, 1)
        a = a * jnp.sum(jnp.where(lane == f, gates, 0.0), axis=-1, keepdims=True)
    acc_sc[...] += _dot(a.astype(BF), wd_ref[...])

    @pl.when(f == pl.num_programs(2) - 1)
    def _():
        o_ref[0] = o_ref[0] + gate_ref[0] * acc_sc[...]


def _ffn(x, gain, shift, scale, gate, w_gate, w_up, w_down, w_router=None, mixer=None):
    B, T, D = x.shape
    tm = min(T, FFN_ROW_TILE)
    moe = w_router is not None
    mixed = mixer is not None
    row = pl.BlockSpec((1, tm, D), lambda b, i, f: (b, i, 0))
    in_specs = [row, _const_spec((1, D)), _mod_spec(shift, tm), _mod_spec(scale, tm), _mod_spec(gate, tm)]
    args = [x, gain.reshape(1, D), shift, scale, gate]
    if mixed:
        a, w_out, mix_gate = mixer
        in_specs += [pl.BlockSpec((1, tm, a.shape[-1]), lambda b, i, f: (b, i, 0)),
                     _const_spec(w_out.shape), _mod_spec(mix_gate, tm)]
        args += [a, w_out, mix_gate]
    scratch = [pltpu.VMEM((tm, D), BF), pltpu.VMEM((tm, D), F32)]
    if moe:
        nf, _, tf = w_gate.shape
        in_specs += [_const_spec((D, LANES)),
                     pl.BlockSpec((None, D, tf), lambda b, i, f: (f, 0, 0)),
                     pl.BlockSpec((None, D, tf), lambda b, i, f: (f, 0, 0)),
                     pl.BlockSpec((None, tf, D), lambda b, i, f: (f, 0, 0))]
        args.append(jnp.pad(w_router, ((0, 0), (0, LANES - N_EXPERTS))))
        scratch.append(pltpu.VMEM((tm, LANES), F32))
    else:
        tf = FFN_COL_TILE
        nf = w_gate.shape[1] // tf
        in_specs += [pl.BlockSpec((D, tf), lambda b, i, f: (0, f)),
                     pl.BlockSpec((D, tf), lambda b, i, f: (0, f)),
                     pl.BlockSpec((tf, D), lambda b, i, f: (f, 0))]
    args += [w_gate, w_up, w_down]
    return pl.pallas_call(
        functools.partial(_ffn_kernel, moe=moe, mixed=mixed),
        grid=(B, T // tm, nf),
        in_specs=in_specs, out_specs=row, scratch_shapes=scratch,
        out_shape=jax.ShapeDtypeStruct((B, T, D), F32),
        compiler_params=_cparams("parallel", "parallel", "arbitrary"),
        name="moe_ffn" if moe else "dense_ffn",
    )(*args)


def _lru_proj_kernel(x_ref, g_ref, sh_ref, sc_ref, w_ref, gate_ref, xr_ref, tail_ref, *, tail_rows):
    h = _normmod(x_ref[0], g_ref[...], sh_ref[0], sc_ref[0])
    hb = h.astype(BF)
    tm = hb.shape[0]
    for c in range(2 * D_RNN // MXU_N):
        acc = _dot(hb, w_ref[:, c * MXU_N:(c + 1) * MXU_N])
        if c * MXU_N < D_RNN:
            gate_ref[0, :, c * MXU_N:(c + 1) * MXU_N] = jax.nn.gelu(acc).astype(BF)
        else:
            sl = slice(c * MXU_N - D_RNN, (c + 1) * MXU_N - D_RNN)
            xr_ref[0, :, sl] = acc.astype(BF)
            tail_ref[0, :, sl] = acc[tm - tail_rows:, :]


def _lru_proj(x, gain, shift, scale, w_in, all_rows_f32):
    B, T, D = x.shape
    tm = min(T, ROW_TILE)
    tail_rows = tm if all_rows_f32 else SUBLANES
    tail_spec = pl.BlockSpec((1, tail_rows, D_RNN), (lambda b, i: (b, i, 0)) if all_rows_f32
                                       else (lambda b, i: (b, 0, 0)))
    row = pl.BlockSpec((1, tm, D), lambda b, i: (b, i, 0))
    rnn = pl.BlockSpec((1, tm, D_RNN), lambda b, i: (b, i, 0))
    return pl.pallas_call(
        functools.partial(_lru_proj_kernel, tail_rows=tail_rows), grid=(B, T // tm),
        in_specs=[row, _const_spec((1, D)), _mod_spec(shift, tm), _mod_spec(scale, tm),
                  _const_spec((D, 2 * D_RNN))],
        out_specs=[rnn, rnn, tail_spec],
        out_shape=[jax.ShapeDtypeStruct((B, T, D_RNN), BF), jax.ShapeDtypeStruct((B, T, D_RNN), BF),
                   jax.ShapeDtypeStruct((B, T if all_rows_f32 else SUBLANES, D_RNN), F32)],
        compiler_params=_cparams("parallel", "arbitrary"),
        name="lru_proj",
    )(x, gain.reshape(1, D), shift, scale, w_in)


def _lru_core_kernel(gg_ref, xr_ref, x_ref, gate_ref, h0_ref, prev_ref, cw_ref, cb_ref, wri_ref, bri_ref,
                     lam_ref, wo_ref, o_ref, hl_ref, xp_sc, a_sc, b_sc, hs_sc, h_sc):
    i = pl.program_id(1)
    tm = xr_ref.shape[1]

    @pl.when(i == 0)
    def _():
        xp_sc[0:SUBLANES, :] = prev_ref[0]
        h_sc[...] = h0_ref[0]

    xp_sc[SUBLANES:, :] = xr_ref[0].astype(F32)
    xc = cb_ref[...]
    for j in range(CONV_W):
        off = SUBLANES - (CONV_W - 1) + j
        xc = xc + cw_ref[j:j + 1, :] * xp_sc[off:off + tm, :]
    xp_sc[0:SUBLANES, :] = xp_sc[tm:tm + SUBLANES, :]

    xcb = xc.astype(BF)
    neg_decay = -LRU_C * _softplus(-lam_ref[...])
    for s in range(D_RNN // RNN_SUPER):
        sl = slice(s * RNN_SUPER, (s + 1) * RNN_SUPER)
        z = _dot(xcb[:, sl], wri_ref[s])
        r = jax.nn.sigmoid(z[:, :RNN_SUPER] + bri_ref[0:1, sl])
        g = jax.nn.sigmoid(z[:, RNN_SUPER:] + bri_ref[1:2, sl])
        log_a = neg_decay[:, sl] * r
        y = 2.0 * log_a
        a_sc[:, sl] = jnp.exp(log_a)
        b_sc[:, sl] = jnp.sqrt(-jnp.tanh(log_a) * (jnp.exp(y) + 1.0)) * (g * xc[:, sl])

    tile_row = lax.broadcasted_iota(jnp.int32, (SUBLANES, D_RNN), 0)

    def scan_tile(k, h):
        rows = pl.ds(pl.multiple_of(k * SUBLANES, SUBLANES), SUBLANES)
        a = a_sc[rows, :]
        b = b_sc[rows, :]
        d = 1
        while d < SUBLANES:
            a_prev = jnp.where(tile_row >= d, pltpu.roll(a, d, axis=0), 1.0)
            b_prev = jnp.where(tile_row >= d, pltpu.roll(b, d, axis=0), 0.0)
            b = a * b_prev + b
            a = a * a_prev
            d *= 2
        hs = a * h + b
        hs_sc[rows, :] = hs
        return hs[SUBLANES - 1:, :]

    h_last = lax.fori_loop(0, tm // SUBLANES, scan_tile, h_sc[...], unroll=2)
    h_sc[...] = h_last
    hl_ref[0] = h_last
    y = (gg_ref[0].astype(F32) * hs_sc[...]).astype(BF)
    o_ref[0] = x_ref[0] + gate_ref[0] * _dot(y, wo_ref[...])


def _block_diag(w):
    n_super = D_RNN // RNN_SUPER
    per = N_RNN_BLOCKS // n_super
    w = w.reshape(n_super, per, w.shape[1], w.shape[2])
    return jnp.einsum('sbij,bc->sbicj', w, jnp.eye(per, dtype=w.dtype)).reshape(n_super, RNN_SUPER, RNN_SUPER)


def _lru_core(gg, xr, x, gate, h0, prev, conv_w, conv_b, w_r, b_r, w_i, b_i, lam, w_out):
    B, T, D = x.shape
    tm = min(T, ROW_TILE)
    w_ri = jnp.concatenate([_block_diag(w_r), _block_diag(w_i)], axis=-1).astype(BF)
    b_ri = jnp.stack([b_r, b_i])
    row = pl.BlockSpec((1, tm, D), lambda b, i: (b, i, 0))
    rnn = pl.BlockSpec((1, tm, D_RNN), lambda b, i: (b, i, 0))
    n_super = D_RNN // RNN_SUPER
    return pl.pallas_call(
        _lru_core_kernel, grid=(B, T // tm),
        in_specs=[rnn, rnn, row, _mod_spec(gate, tm),
                  pl.BlockSpec((1, 1, D_RNN), lambda b, i: (b, 0, 0)),
                  pl.BlockSpec((1, SUBLANES, D_RNN), lambda b, i: (b, 0, 0)),
                  _const_spec((CONV_W, D_RNN)), _const_spec((1, D_RNN)),
                  _const_spec((n_super, RNN_SUPER, 2 * RNN_SUPER)), _const_spec((2, D_RNN)),
                  _const_spec((1, D_RNN)), _const_spec((D_RNN, D))],
        out_specs=[row, pl.BlockSpec((1, 1, D_RNN), lambda b, i: (b, 0, 0))],
        out_shape=[jax.ShapeDtypeStruct((B, T, D), F32), jax.ShapeDtypeStruct((B, 1, D_RNN), F32)],
        scratch_shapes=[pltpu.VMEM((tm + SUBLANES, D_RNN), F32), pltpu.VMEM((tm, D_RNN), F32),
                        pltpu.VMEM((tm, D_RNN), F32), pltpu.VMEM((tm, D_RNN), F32),
                        pltpu.VMEM((1, D_RNN), F32)],
        compiler_params=_cparams("parallel", "arbitrary"),
        name="lru_core",
    )(gg, xr, x, gate, h0, prev, conv_w, conv_b.reshape(1, D_RNN), w_ri, b_ri, lam.reshape(1, D_RNN), w_out)


def _trunk(x, mod, W, past):
    B, T, D = x.shape
    depth = mod.shape[0]
    has_past = past is not None
    flat = (lambda a: a.reshape(1, B * T, a.shape[-1])) if has_past else (lambda a: a)
    unflat = (lambda a: a.reshape(B, T, a.shape[-1])) if has_past else (lambda a: a)
    n_fox = len(range(0, depth, N_MIXERS))
    n_sb = len(range(2, depth, N_MIXERS))
    fox_kv = sb_kv = None
    fox_logf, lru_new = [], []
    for i in range(depth):
        per_batch = [m[:, None, :] for m in jnp.split(mod[i], 6, axis=-1)]
        if has_past:
            per_row = [jnp.broadcast_to(m, (B, T, D)).reshape(1, B * T, D) for m in per_batch]
        else:
            per_row = per_batch
        shift1, scale1, gate1, shift2, scale2, gate2 = per_row
        kind, j = i % N_MIXERS, i // N_MIXERS
        if kind == 0:
            fox_params = (W['fox_w_f'][j], W['fox_b_f'][j], W['fox_q_norm'][j], W['fox_k_norm'][j])
            q, k32, kbf, v32, vbf, logf = _qkv_proj(flat(x), W['norm_mix'][i], shift1, scale1,
                                                    W['fox_w_qkv'][j], fox_params, n_fox, j, fox_kv)
            fox_kv = (k32, v32)
            q, kbf, vbf, logf = (unflat(a) for a in (q, kbf, vbf, logf))
            fox_logf.append(logf[..., :N_HEADS])
            if has_past:
                cache_k, cache_v, cache_lf = past['fox']
                Tp = cache_k.shape[2]
                lf_all = jnp.concatenate([jnp.swapaxes(cache_lf[j], 1, 2),
                                          jnp.swapaxes(fox_logf[-1], 1, 2)], axis=-1)
                pad = (-lf_all.shape[-1]) % LANES
                cum = _cumsum_flat(jnp.pad(lf_all, ((0, 0), (0, 0), (0, pad))).reshape(B * N_HEADS, -1))
                cum = cum.reshape(B, N_HEADS, -1) * LOG2E
                o = _fox_attention_past(q, kbf, vbf, cache_k.reshape(*cache_k.shape[:3], D),
                                        cache_v.reshape(*cache_v.shape[:3], D), j,
                                        cum[..., :Tp], cum[..., Tp:Tp + T])
            else:
                bound = 1.02 * HEAD_DIM ** 0.5 * jnp.max(jnp.abs(W['fox_q_norm'][j])) \
                    * jnp.max(jnp.abs(W['fox_k_norm'][j]))
                o = _fox_attention(q, kbf, vbf, logf, bound, min(T, FOX_BLOCK))
            mixer = (flat(o), W['fox_w_out'][j], gate1)
        elif kind == 1:
            gg, xr, tail = _lru_proj(flat(x), W['norm_mix'][i], shift1, scale1, W['lru_w_in'][j], has_past)
            gg, xr, tail = unflat(gg), unflat(xr), unflat(tail) if has_past else tail
            if has_past:
                state_h, state_conv = past['lru']
                h0 = state_h[j][:, None, :]
                prev = jnp.pad(state_conv[j], ((0, 0), (SUBLANES - (CONV_W - 1), 0), (0, 0)))
            else:
                h0 = jnp.zeros((B, 1, D_RNN), F32)
                prev = jnp.zeros((B, SUBLANES, D_RNN), F32)
            x, h_last = _lru_core(gg, xr, x, per_batch[2], h0, prev, W['lru_conv_w'][j], W['lru_conv_b'][j],
                                  W['lru_w_r'][j], W['lru_b_r'][j], W['lru_w_i'][j], W['lru_b_i'][j],
                                  W['lru_lambda'][j], W['lru_w_out'][j])
            lru_new.append((h_last[:, 0], tail[:, tail.shape[1] - (CONV_W - 1):]))
            mixer = None
        else:
            q, k32, kbf, v32, vbf = _qkv_proj(flat(x), W['norm_mix'][i], shift1, scale1, W['sb_w_in'][j],
                                              None, n_sb, j, sb_kv)
            sb_kv = (k32, v32)
            q, kbf, vbf = (unflat(a) for a in (q, kbf, vbf))
            if has_past:
                cache_k, cache_v = past['sb']
                o = _sb_attention_past(q, kbf, vbf, cache_k.reshape(*cache_k.shape[:3], D),
                                       cache_v.reshape(*cache_v.shape[:3], D), j)
            else:
                o = _sb_attention(q, kbf, vbf, min(T, SB_BLOCK))
            mixer = (flat(o), W['sb_w_out'][j], gate1)
        if i % 2 == 0:
            x = unflat(_ffn(flat(x), W['norm_ffn'][i], shift2, scale2, gate2,
                            W['ffn_w_gate'][i // 2], W['ffn_w_up'][i // 2], W['ffn_w_down'][i // 2],
                            None, mixer))
        else:
            x = unflat(_ffn(flat(x), W['norm_ffn'][i], shift2, scale2, gate2,
                            W['moe_w_gate'][i // 2], W['moe_w_up'][i // 2], W['moe_w_down'][i // 2],
                            W['moe_w_router'][i // 2], mixer))

    def heads(state):
        return state.reshape(state.shape[0], B, T, N_HEADS, HEAD_DIM)

    fox_state = [heads(fox_kv[0]), heads(fox_kv[1]), jnp.stack(fox_logf)]
    lru_state = [jnp.stack([s[n] for s in lru_new]) for n in range(2)]
    sb_state = [heads(sb_kv[0]), heads(sb_kv[1])]
    return x, fox_state, lru_state, sb_state


MATMUL_WEIGHTS = ('fox_w_qkv', 'fox_w_out', 'lru_w_in', 'lru_w_out', 'sb_w_in', 'sb_w_out',
                  'ffn_w_gate', 'ffn_w_up', 'ffn_w_down', 'moe_w_gate', 'moe_w_up', 'moe_w_down')


def _prepare_weights(W):
    W = dict(W)
    w_in = W.pop('fox_w_in')
    W['fox_w_qkv'] = w_in[:, :, :3 * D_MODEL]
    W['fox_w_f'] = w_in[:, :, 3 * D_MODEL:]
    for name in MATMUL_WEIGHTS:
        W[name] = W[name].astype(BF)
    return W


def _forward(W, x_prompt, c_prompt, x_sample, c_sample, past):
    W = _prepare_weights(W)
    n_prompt, n_sample = c_prompt.shape[0], c_sample.shape[0]
    c_all = jnp.concatenate([c_prompt, c_sample], axis=0)
    c_all = jnp.pad(c_all, ((0, (-c_all.shape[0]) % (2 * SUBLANES)), (0, 0)))
    mod = _ada(c_all, W['w_ada'], W['b_ada'])
    y_p, fox_p, lru_p, sb_p = _trunk(x_prompt, mod[:, :n_prompt], W, None)
    y_s, fox_s, lru_s, sb_s = _trunk(x_sample, mod[:, n_prompt:n_prompt + n_sample], W, past)

    return (y_p, y_s, *fox_p, *fox_s, *lru_p, *lru_s, *sb_p, *sb_s)


def kernel(x_prompt, x_sample, cache_fox_k, cache_fox_v, cache_fox_logf, state_lru_h, state_lru_conv,
           cache_sb_k, cache_sb_v, c_prompt, c_sample, norm_mix, norm_ffn, w_ada, b_ada,
           fox_w_in, fox_b_f, fox_q_norm, fox_k_norm, fox_w_out,
           lru_w_in, lru_conv_w, lru_conv_b, lru_w_r, lru_b_r, lru_w_i, lru_b_i, lru_lambda, lru_w_out,
           sb_w_in, sb_w_out, ffn_w_gate, ffn_w_up, ffn_w_down,
           moe_w_router, moe_w_gate, moe_w_up, moe_w_down):
    W = {
        'norm_mix': norm_mix, 'norm_ffn': norm_ffn, 'w_ada': w_ada, 'b_ada': b_ada,
        'fox_w_in': fox_w_in, 'fox_b_f': fox_b_f, 'fox_q_norm': fox_q_norm,
        'fox_k_norm': fox_k_norm, 'fox_w_out': fox_w_out,
        'lru_w_in': lru_w_in, 'lru_conv_w': lru_conv_w, 'lru_conv_b': lru_conv_b,
        'lru_w_r': lru_w_r, 'lru_b_r': lru_b_r, 'lru_w_i': lru_w_i, 'lru_b_i': lru_b_i,
        'lru_lambda': lru_lambda, 'lru_w_out': lru_w_out,
        'sb_w_in': sb_w_in, 'sb_w_out': sb_w_out,
        'ffn_w_gate': ffn_w_gate, 'ffn_w_up': ffn_w_up, 'ffn_w_down': ffn_w_down,
        'moe_w_router': moe_w_router, 'moe_w_gate': moe_w_gate, 'moe_w_up': moe_w_up,
        'moe_w_down': moe_w_down,
    }
    past = {'fox': (cache_fox_k, cache_fox_v, cache_fox_logf), 'lru': (state_lru_h, state_lru_conv),
            'sb': (cache_sb_k, cache_sb_v)}
    return _forward(W, x_prompt, c_prompt, x_sample, c_sample, past)
```

```python
import functools

import jax
import jax.numpy as jnp
from jax import lax
from jax.experimental import pallas as pl
from jax.experimental.pallas import tpu as pltpu

F32 = jnp.float32
BF = jnp.bfloat16
HIGHEST = lax.Precision.HIGHEST

D_MODEL = 1024
N_HEADS = 8
HEAD_DIM = 128
N_MIXERS = 3
D_RNN = 1280
N_RNN_BLOCKS = 16
RNN_SUPER = 640
CONV_W = 4
LRU_C = 8.0
N_EXPERTS = 8
EPS = 1e-6
ATTN_SCALE = HEAD_DIM ** -0.5
LANES = 128
SUBLANES = 8
MXU_N = 256
EXP_UNDERFLOW = 104.0
EXP2_UNDERFLOW = 150.0
LOG2E = 1.4426950408889634
Q_SCALE = ATTN_SCALE * LOG2E
EXP2_FLUSH = 126.0
FIXED_REF_MAX_BOUND = 24.0
VMEM_LIMIT = 56 * 2 ** 20

FOX_BLOCK = 512
SB_BLOCK = 256
ROW_TILE = 512
FFN_ROW_TILE = 1024
FFN_COL_TILE = 512


def _cparams(*sem):
    return pltpu.CompilerParams(dimension_semantics=sem, vmem_limit_bytes=VMEM_LIMIT)


def _dot(a, b, **kw):
    return jnp.dot(a, b, preferred_element_type=F32, **kw)


def _dot_nt(a, b):
    return lax.dot_general(a, b, (((1,), (1,)), ((), ())), preferred_element_type=F32)


def _normmod(x, gain, shift, scale):
    ms = jnp.mean(x * x, axis=-1, keepdims=True)
    return x * lax.rsqrt(ms + EPS) * gain * (1.0 + scale) + shift


def _softplus(x):
    return jnp.maximum(x, 0.0) + jnp.log1p(jnp.exp(-jnp.abs(x)))


def _mod_spec(mod, tm):
    d = mod.shape[-1]
    if mod.shape[1] == 1:
        return pl.BlockSpec((1, 1, d), lambda b, i, *_: (b, 0, 0))
    return pl.BlockSpec((1, tm, d), lambda b, i, *_: (b, i, 0))


def _const_spec(shape):
    zeros = (0,) * len(shape)
    return pl.BlockSpec(shape, lambda *_: zeros)


def _ada_kernel(c_ref, w_ref, b_ref, o_ref):
    c = c_ref[...]
    c_act = (c * jax.nn.sigmoid(c)).astype(BF)
    o_ref[0] = _dot(c_act, w_ref[0].astype(BF)) + b_ref[0]


def _ada(c_all, w_ada, b_ada):
    rows, d = c_all.shape
    depth, _, n = w_ada.shape
    tn = 1024
    return pl.pallas_call(
        _ada_kernel,
        grid=(depth, n // tn),
        in_specs=[pl.BlockSpec((rows, d), lambda l, j: (0, 0)),
                  pl.BlockSpec((1, d, tn), lambda l, j: (l, 0, j)),
                  pl.BlockSpec((1, 1, tn), lambda l, j: (l, 0, j))],
        out_specs=pl.BlockSpec((1, rows, tn), lambda l, j: (l, 0, j)),
        out_shape=jax.ShapeDtypeStruct((depth, rows, n), F32),
        compiler_params=_cparams("parallel", "parallel"),
        name="ada_modulation",
    )(c_all, w_ada, b_ada.reshape(depth, 1, n))


def _qkv_kernel(*refs, fox, aliased):
    refs = list(refs)
    x_ref, g_ref, sh_ref, sc_ref, w_ref = refs[:5]
    del refs[:5]
    if fox:
        wf_ref, bf_ref, qg_ref, kg_ref = refs[:4]
        del refs[:4]
    if aliased:
        del refs[:2]
    q_ref, k32_ref, kbf_ref, v32_ref, vbf_ref = refs[:5]
    h = _normmod(x_ref[0], g_ref[...], sh_ref[0], sc_ref[0])
    hb = h.astype(BF)
    tm = hb.shape[0]

    def head_norm(a, gain_ref):
        ms = jnp.mean(a * a, axis=-1, keepdims=True)
        return a * lax.rsqrt(ms + EPS) * gain_ref[...]

    for c in range(3 * D_MODEL // MXU_N):
        acc = _dot(hb, w_ref[:, c * MXU_N:(c + 1) * MXU_N])
        part = (c * MXU_N) // D_MODEL
        for hh in range(MXU_N // HEAD_DIM):
            col = (c * MXU_N) % D_MODEL + hh * HEAD_DIM
            sl = slice(col, col + HEAD_DIM)
            state_rows = pl.ds(col // HEAD_DIM, tm, stride=N_HEADS)
            a = acc[:, hh * HEAD_DIM:(hh + 1) * HEAD_DIM]
            if part == 0:
                if fox:
                    a = head_norm(a, qg_ref)
                q_ref[0, :, sl] = (a * Q_SCALE).astype(BF)
            elif part == 1:
                if fox:
                    a = head_norm(a, kg_ref)
                k32_ref[0, state_rows, :] = a
                kbf_ref[0, :, sl] = a.astype(BF)
            else:
                v32_ref[0, state_rows, :] = a
                vbf_ref[0, :, sl] = a.astype(BF)
    if fox:
        f_logit = _dot(h, wf_ref[...], precision=HIGHEST) + bf_ref[...]
        refs[5][0] = -_softplus(-f_logit)


def _qkv_proj(x, gain, shift, scale, w_qkv, fox_params=None, n_slots=1, slot=0, state=None):
    B, T, D = x.shape
    tm = min(T, ROW_TILE)
    fox = fox_params is not None
    aliased = state is not None
    row = pl.BlockSpec((1, tm, D), lambda b, i: (b, i, 0))
    state_spec = pl.BlockSpec((None, 1, tm * N_HEADS, HEAD_DIM), lambda b, i: (slot, b, i, 0))
    state_shape = jax.ShapeDtypeStruct((n_slots, B, T * N_HEADS, HEAD_DIM), F32)
    in_specs = [row, _const_spec((1, D)), _mod_spec(shift, tm), _mod_spec(scale, tm), _const_spec((D, 3 * D))]
    args = [x, gain.reshape(1, D), shift, scale, w_qkv]
    out_specs = [row, state_spec, row, state_spec, row]
    out_shape = [jax.ShapeDtypeStruct((B, T, D), BF), state_shape, jax.ShapeDtypeStruct((B, T, D), BF),
                 state_shape, jax.ShapeDtypeStruct((B, T, D), BF)]
    if fox:
        w_f, b_f, q_gain, k_gain = fox_params
        in_specs += [_const_spec((D, LANES)), _const_spec((1, LANES)),
                     _const_spec((1, HEAD_DIM)), _const_spec((1, HEAD_DIM))]
        args += [jnp.pad(w_f, ((0, 0), (0, LANES - N_HEADS))),
                 jnp.pad(b_f, (0, LANES - N_HEADS)).reshape(1, LANES),
                 q_gain.reshape(1, HEAD_DIM), k_gain.reshape(1, HEAD_DIM)]
        out_specs.append(pl.BlockSpec((1, tm, LANES), lambda b, i: (b, i, 0)))
        out_shape.append(jax.ShapeDtypeStruct((B, T, LANES), F32))
    aliases = {}
    if aliased:
        aliases = {len(args): 1, len(args) + 1: 3}
        in_specs += [pl.BlockSpec(memory_space=pl.ANY)] * 2
        args += list(state)
    return pl.pallas_call(
        functools.partial(_qkv_kernel, fox=fox, aliased=aliased),
        grid=(B, T // tm),
        in_specs=in_specs, out_specs=out_specs, out_shape=out_shape,
        input_output_aliases=aliases,
        compiler_params=_cparams("parallel", "parallel"),
        name="fox_qkv_proj" if fox else "sb_qkv_proj",
    )(*args)


def _split3_bf16(x):
    hi = x.astype(BF).astype(F32)
    mid = (x - hi).astype(BF).astype(F32)
    return hi, mid, (x - hi - mid).astype(BF).astype(F32)


def _fox_prep_kernel(bound_ref, lf_ref, qx_ref, kx_ref, cum_ref, bmin_ref, bmax_ref, carry_sc):
    i = pl.program_id(1)
    tm = lf_ref.shape[1]

    @pl.when(i == 0)
    def _():
        carry_sc[...] = jnp.zeros_like(carry_sc)

    r = lax.broadcasted_iota(jnp.int32, (tm, tm), 0)
    c = lax.broadcasted_iota(jnp.int32, (tm, tm), 1)
    cum = _dot((c <= r).astype(F32), lf_ref[0], precision=HIGHEST) + carry_sc[...]
    carry_sc[...] = cum[tm - 1:tm, :]
    cum_ref[0] = cum
    bmin_ref[0, 0] = jnp.min(cum, axis=0, keepdims=True)
    bmax_ref[0, 0] = jnp.max(cum, axis=0, keepdims=True)
    c2 = cum * LOG2E
    bound2 = bound_ref[0] * LOG2E
    lane = lax.broadcasted_iota(jnp.int32, (tm, LANES), 1)
    for h in range(N_HEADS):
        col = jnp.broadcast_to(c2[:, h:h + 1], (tm, LANES))
        k1, k2, k3 = _split3_bf16(-col)
        q1, q2, q3 = _split3_bf16(col - bound2)
        kx = jnp.where(lane == 0, k1, jnp.where(lane == 1, k2, jnp.where(lane == 2, k3,
                                                                         jnp.where(lane < 6, 1.0, 0.0))))
        qx = jnp.where(lane < 3, 1.0, jnp.where(lane == 3, q1, jnp.where(lane == 4, q2,
                                                                         jnp.where(lane == 5, q3, 0.0))))
        sl = slice(h * LANES, (h + 1) * LANES)
        qx_ref[0, :, sl] = qx.astype(BF)
        kx_ref[0, :, sl] = kx.astype(BF)


def _fox_prep(logf, bound, blk):
    B, T, _ = logf.shape
    nb = T // blk
    row = pl.BlockSpec((1, blk, LANES), lambda b, i: (b, i, 0))
    wide = pl.BlockSpec((1, blk, N_HEADS * LANES), lambda b, i: (b, i, 0))
    stat = pl.BlockSpec((1, 1, 1, LANES), lambda b, i: (b, i, 0, 0))
    return pl.pallas_call(
        _fox_prep_kernel, grid=(B, nb),
        in_specs=[pl.BlockSpec(memory_space=pltpu.SMEM), row],
        out_specs=[wide, wide, row, stat, stat],
        out_shape=[jax.ShapeDtypeStruct((B, T, N_HEADS * LANES), BF)] * 2
        + [jax.ShapeDtypeStruct((B, T, LANES), F32)] + [jax.ShapeDtypeStruct((B, nb, 1, LANES), F32)] * 2,
        scratch_shapes=[pltpu.VMEM((1, LANES), F32)],
        compiler_params=_cparams("parallel", "arbitrary"),
        name="fox_prep",
    )(bound.reshape(1), logf)


def _cumsum_flat_kernel(x_ref, o_ref):
    n = x_ref.shape[-1]
    r = lax.broadcasted_iota(jnp.int32, (n, n), 0)
    c = lax.broadcasted_iota(jnp.int32, (n, n), 1)
    o_ref[...] = _dot(x_ref[...], (r <= c).astype(F32), precision=HIGHEST)


def _cumsum_flat(x):
    return pl.pallas_call(
        _cumsum_flat_kernel, out_shape=jax.ShapeDtypeStruct(x.shape, F32),
        compiler_params=pltpu.CompilerParams(vmem_limit_bytes=VMEM_LIMIT),
        name="cumsum_flat",
    )(x)


def _fox_attn_kernel(js_ref, q_ref, qx_ref, k_ref, kx_ref, v_ref, o_ref, acc_sc, *, blk):
    b, h, i = pl.program_id(0), pl.program_id(1), pl.program_id(2)
    nq = pl.num_programs(2)
    q = jnp.concatenate([q_ref[0], qx_ref[0]], axis=-1)
    acc_sc[...] = jnp.zeros_like(acc_sc)
    lane = lax.broadcasted_iota(jnp.int32, (blk, HEAD_DIM), 1)
    ones_col = (lane == 0).astype(BF)

    def block(j, diagonal):
        start = pl.multiple_of(j * blk, blk)
        kb = jnp.concatenate([k_ref[0, pl.ds(start, blk), :], kx_ref[0, pl.ds(start, blk), :]], axis=-1)
        vb = jnp.concatenate([v_ref[0, pl.ds(start, blk), :], ones_col], axis=-1)
        s = _dot_nt(q, kb)
        if diagonal:
            row = lax.broadcasted_iota(jnp.int32, (blk, blk), 0)
            col = lax.broadcasted_iota(jnp.int32, (blk, blk), 1)
            s = jnp.where(col <= row, s, -jnp.inf)
        acc_sc[...] += _dot(jnp.exp2(s).astype(BF), vb)

    j0 = js_ref[(b * N_HEADS + h) * nq + i]
    n_before = i - j0

    def pair(t, carry):
        block(j0 + 2 * t, False)
        block(j0 + 2 * t + 1, False)
        return carry

    lax.fori_loop(0, n_before // 2, pair, 0)

    @pl.when(n_before % 2 == 1)
    def _():
        block(i - 1, False)
        block(i, True)

    @pl.when(n_before % 2 == 0)
    def _():
        block(i, True)

    acc = acc_sc[...]
    o_ref[0] = (acc[:, :HEAD_DIM] / acc[:, HEAD_DIM:HEAD_DIM + 1]).astype(BF)


def _fox_attn_online_kernel(js_ref, q_ref, k_ref, v_ref, ck_ref, o_ref, m_sc, l_sc, acc_sc, *, blk):
    b, h, i = pl.program_id(0), pl.program_id(1), pl.program_id(2)
    nq = pl.num_programs(2)
    q = q_ref[0]
    m_sc[...] = jnp.full_like(m_sc, -jnp.inf)
    l_sc[...] = jnp.zeros_like(l_sc)
    acc_sc[...] = jnp.zeros_like(acc_sc)

    def block(j, diagonal):
        start = pl.multiple_of(j * blk, blk)
        kb = k_ref[0, pl.ds(start, blk), :]
        vb = v_ref[0, pl.ds(start, blk), :]
        s = _dot_nt(q, kb) - ck_ref[0, j]
        if diagonal:
            row = lax.broadcasted_iota(jnp.int32, (blk, blk), 0)
            col = lax.broadcasted_iota(jnp.int32, (blk, blk), 1)
            s = jnp.where(col <= row, s, -jnp.inf)
        m_prev = m_sc[...]
        m_new = jnp.maximum(m_prev, jnp.max(s, axis=-1, keepdims=True))
        alpha = jnp.exp2(m_prev - m_new)
        p = jnp.exp2(s - m_new)
        l_sc[...] = alpha * l_sc[...] + jnp.sum(p, axis=-1, keepdims=True)
        acc_sc[...] = alpha * acc_sc[...] + _dot(p.astype(BF), vb)
        m_sc[...] = m_new

    def body(j, carry):
        block(j, False)
        return carry

    lax.fori_loop(js_ref[(b * N_HEADS + h) * nq + i], i, body, 0)
    block(i, True)
    o_ref[0] = (acc_sc[...] / l_sc[...]).astype(BF)


def _fox_first_blocks(block_min, block_max, dead_gap):
    nb = block_min.shape[-1]
    j = jnp.arange(nb)
    upto = j[None, :] <= j[:, None]
    prefix_min = jnp.min(jnp.where(upto, block_min[..., None, :], jnp.inf), axis=-1)
    dead = (prefix_min[..., None, :] - block_max[..., :, None]) > dead_gap
    return jnp.sum(dead & (j[None, :] < j[:, None]), axis=-1).astype(jnp.int32)


def _fox_attention(q, kbf, vbf, logf, logit_bound, blk):
    B, T, D = q.shape
    nq = T // blk
    qx, kx, cum, bmin, bmax = _fox_prep(logf, logit_bound, blk)
    heads_first = lambda a: jnp.swapaxes(a[:, :, 0, :N_HEADS], 1, 2)
    use_fixed_reference = logit_bound < FIXED_REF_MAX_BOUND
    dead_gap = jnp.where(use_fixed_reference, (EXP2_FLUSH + 1.0) / LOG2E,
                         2.0 * logit_bound + EXP_UNDERFLOW + 1.0)
    first_blocks = _fox_first_blocks(heads_first(bmin), heads_first(bmax), dead_gap).reshape(-1)
    qspec = pl.BlockSpec((1, blk, HEAD_DIM), lambda b, h, i, js: (b, i, h))
    kvspec = pl.BlockSpec((1, T, HEAD_DIM), lambda b, h, i, js: (b, 0, h))
    out_shape = jax.ShapeDtypeStruct((B, T, D), BF)

    def fixed_reference():
        return pl.pallas_call(
            functools.partial(_fox_attn_kernel, blk=blk),
            grid_spec=pltpu.PrefetchScalarGridSpec(
                num_scalar_prefetch=1, grid=(B, N_HEADS, nq),
                in_specs=[qspec, qspec, kvspec, kvspec, kvspec], out_specs=qspec,
                scratch_shapes=[pltpu.VMEM((blk, 2 * HEAD_DIM), F32)]),
            out_shape=out_shape,
            compiler_params=_cparams("parallel", "parallel", "arbitrary"),
            name="fox_attention",
        )(first_blocks, q, qx, kbf, kx, vbf)

    def online():
        ck = jnp.swapaxes(cum[:, :, :N_HEADS], 1, 2) * LOG2E
        ckspec = pl.BlockSpec((1, nq, 1, blk), lambda b, h, i, js: (b * N_HEADS + h, 0, 0, 0))
        return pl.pallas_call(
            functools.partial(_fox_attn_online_kernel, blk=blk),
            grid_spec=pltpu.PrefetchScalarGridSpec(
                num_scalar_prefetch=1, grid=(B, N_HEADS, nq),
                in_specs=[qspec, kvspec, kvspec, ckspec], out_specs=qspec,
                scratch_shapes=[pltpu.VMEM((blk, 1), F32), pltpu.VMEM((blk, 1), F32),
                                pltpu.VMEM((blk, HEAD_DIM), F32)]),
            out_shape=out_shape,
            compiler_params=_cparams("parallel", "parallel", "arbitrary"),
            name="fox_attention_online",
        )(first_blocks, q, kbf, vbf, ck.reshape(B * N_HEADS, nq, 1, blk))

    return lax.cond(use_fixed_reference, fixed_reference, online)


def _fox_attn_past_kernel(q_ref, kn_ref, vn_ref, kp_ref, vp_ref, cp_ref, cn_ref, o_ref):
    tn = q_ref.shape[1]
    row = lax.broadcasted_iota(jnp.int32, (tn, tn), 0)
    col = lax.broadcasted_iota(jnp.int32, (tn, tn), 1)
    for h in range(N_HEADS):
        sl = slice(h * HEAD_DIM, (h + 1) * HEAD_DIM)
        q = q_ref[0, :, sl]
        sp = _dot_nt(q, kp_ref[0, :, sl].astype(BF)) - cp_ref[0, h:h + 1, :]
        sn = _dot_nt(q, kn_ref[0, :, sl]) - cn_ref[0, h:h + 1, :]
        sn = jnp.where(col <= row, sn, -jnp.inf)
        m = jnp.maximum(jnp.max(sp, axis=-1, keepdims=True), jnp.max(sn, axis=-1, keepdims=True))
        pp = jnp.exp2(sp - m)
        pn = jnp.exp2(sn - m)
        l = jnp.sum(pp, axis=-1, keepdims=True) + jnp.sum(pn, axis=-1, keepdims=True)
        o = _dot(pp.astype(BF), vp_ref[0, :, sl].astype(BF)) + _dot(pn.astype(BF), vn_ref[0, :, sl])
        o_ref[0, :, sl] = (o / l).astype(BF)


def _fox_attention_past(q, kn, vn, cache_k, cache_v, layer, cum_past, cum_new):
    B, Tn, D = q.shape
    Tp = cache_k.shape[2]
    new = pl.BlockSpec((1, Tn, D), lambda b: (b, 0, 0))
    past = pl.BlockSpec((None, 1, Tp, D), lambda b: (layer, b, 0, 0))
    return pl.pallas_call(
        _fox_attn_past_kernel, grid=(B,),
        in_specs=[new, new, new, past, past,
                  pl.BlockSpec((1, N_HEADS, Tp), lambda b: (b, 0, 0)),
                  pl.BlockSpec((1, N_HEADS, Tn), lambda b: (b, 0, 0))],
        out_specs=new, out_shape=jax.ShapeDtypeStruct((B, Tn, D), BF),
        compiler_params=_cparams("parallel"),
        name="fox_attention_past",
    )(q, kn, vn, cache_k, cache_v, cum_past, cum_new)


def _split_bf16(x):
    hi = x.astype(BF)
    return hi, (x - hi.astype(F32)).astype(BF)


def _neg_log2_rest(z2):
    zc = jnp.minimum(z2, 30.0)
    return jnp.log(1.0 + jnp.exp2(zc)) * LOG2E + (z2 - zc)


class _SbBlock:
    def __init__(self, blk):
        self.row = lax.broadcasted_iota(jnp.int32, (blk, blk), 0)
        self.col = lax.broadcasted_iota(jnp.int32, (blk, blk), 1)
        suffix = jnp.where(self.row >= self.col, -1.0, 0.0).astype(BF)
        self.suffix2 = jnp.concatenate([suffix, suffix], axis=0)

    def scores(self, q, kb, diagonal):
        z2 = _dot_nt(q, kb)
        nlr = _neg_log2_rest(z2)
        if diagonal:
            nlr = jnp.where(self.col < self.row, nlr, 0.0)
        hi, lo = _split_bf16(nlr)
        tail = _dot(jnp.concatenate([hi, lo], axis=-1), self.suffix2)
        return z2 + tail, tail[:, 0:1]

    def weighted(self, e, later, vb, diagonal):
        a = jnp.exp2(e + later)
        if diagonal:
            a = jnp.where(self.col < self.row, a, 0.0)
        return _dot(a.astype(BF), vb)


def _sb_attn_kernel(q_ref, k_ref, v_ref, o_ref, later_sc, acc_sc, *, blk):
    i = pl.program_id(2)
    q = q_ref[0]
    tile = _SbBlock(blk)

    def kv(ref, j):
        return ref[0, pl.ds(pl.multiple_of(j * blk, blk), blk), :]

    @pl.when(i == 0)
    def _():
        e, total = tile.scores(q, kv(k_ref, 0), True)
        acc_sc[...] = tile.weighted(e, 0.0, kv(v_ref, 0), True)
        later_sc[...] = total

    @pl.when(i > 0)
    def _():
        e_d, total_d = tile.scores(q, kv(k_ref, i), True)
        e_p, total_p = tile.scores(q, kv(k_ref, i - 1), False)
        acc_sc[...] = (tile.weighted(e_d, 0.0, kv(v_ref, i), True)
                       + tile.weighted(e_p, total_d, kv(v_ref, i - 1), False))
        later_sc[...] = total_d + total_p

    def alive(live):
        return live > -EXP2_UNDERFLOW

    def pair(carry):
        j, _ = carry
        e_a, total_a = tile.scores(q, kv(k_ref, j), False)
        e_b, total_b = tile.scores(q, kv(k_ref, j - 1), False)
        later = later_sc[...]
        acc_sc[...] += (tile.weighted(e_a, later, kv(v_ref, j), False)
                        + tile.weighted(e_b, later + total_a, kv(v_ref, j - 1), False))
        later_sc[...] = later + total_a + total_b
        return j - 2, jnp.max(later_sc[...])

    def single(carry):
        j, _ = carry
        e, total = tile.scores(q, kv(k_ref, j), False)
        later = later_sc[...]
        acc_sc[...] += tile.weighted(e, later, kv(v_ref, j), False)
        later_sc[...] = later + total
        return j - 1, jnp.max(later_sc[...])

    carry = lax.while_loop(lambda c: jnp.logical_and(c[0] >= 1, alive(c[1])), pair,
                           (i - 2, jnp.max(later_sc[...])))
    lax.while_loop(lambda c: jnp.logical_and(c[0] >= 0, alive(c[1])), single, carry)
    o_ref[0] = acc_sc[...].astype(BF)


def _sb_attention(q, kbf, vbf, blk):
    B, T, D = q.shape
    qspec = pl.BlockSpec((1, blk, HEAD_DIM), lambda b, h, i: (b, i, h))
    kvspec = pl.BlockSpec((1, T, HEAD_DIM), lambda b, h, i: (b, 0, h))
    return pl.pallas_call(
        functools.partial(_sb_attn_kernel, blk=blk),
        grid=(B, N_HEADS, T // blk),
        in_specs=[qspec, kvspec, kvspec], out_specs=qspec,
        scratch_shapes=[pltpu.VMEM((blk, 1), F32), pltpu.VMEM((blk, HEAD_DIM), F32)],
        out_shape=jax.ShapeDtypeStruct((B, T, D), BF),
        compiler_params=_cparams("parallel", "parallel", "arbitrary"),
        name="sb_attention",
    )(q, kbf, vbf)


def _sb_attn_past_kernel(q_ref, kn_ref, vn_ref, kp_ref, vp_ref, o_ref, zp_sc, lrp_sc, zn_sc, lrn_sc):
    tn = q_ref.shape[1]
    tp = kp_ref.shape[1]
    row = lax.broadcasted_iota(jnp.int32, (tn, tn), 0)
    col = lax.broadcasted_iota(jnp.int32, (tn, tn), 1)
    for h in range(N_HEADS):
        sl = slice(h * HEAD_DIM, (h + 1) * HEAD_DIM)
        rows = slice(h * tn, (h + 1) * tn)
        q = q_ref[0, :, sl]
        zp = _dot_nt(q, kp_ref[0, :, sl].astype(BF))
        zn = _dot_nt(q, kn_ref[0, :, sl])
        zp_sc[rows, :] = zp
        lrp_sc[rows, :] = -_neg_log2_rest(zp)
        zn_sc[rows, :] = zn
        lrn_sc[rows, :] = jnp.where(col < row, -_neg_log2_rest(zn), 0.0)
    rp = lax.broadcasted_iota(jnp.int32, (tp, tp), 0)
    cp = lax.broadcasted_iota(jnp.int32, (tp, tp), 1)
    lrp = lrp_sc[...]
    lrn = lrn_sc[...]
    hi, lo = _split_bf16(lrp)
    later_p = (rp > cp).astype(BF)
    between_p = _dot(hi, later_p) + _dot(lo, later_p) + jnp.sum(lrn, axis=-1, keepdims=True)
    hi, lo = _split_bf16(lrn)
    later_n = (row > col).astype(BF)
    between_n = _dot(hi, later_n) + _dot(lo, later_n)
    ap = jnp.exp2(zp_sc[...] + lrp + between_p).astype(BF)
    an = jnp.exp2(zn_sc[...] + lrn + between_n)
    for h in range(N_HEADS):
        sl = slice(h * HEAD_DIM, (h + 1) * HEAD_DIM)
        rows = slice(h * tn, (h + 1) * tn)
        a_new = jnp.where(col < row, an[rows, :], 0.0).astype(BF)
        o = _dot(ap[rows, :], vp_ref[0, :, sl].astype(BF)) + _dot(a_new, vn_ref[0, :, sl])
        o_ref[0, :, sl] = o.astype(BF)


def _sb_attention_past(q, kn, vn, cache_k, cache_v, layer):
    B, Tn, D = q.shape
    Tp = cache_k.shape[2]
    new = pl.BlockSpec((1, Tn, D), lambda b: (b, 0, 0))
    past = pl.BlockSpec((None, 1, Tp, D), lambda b: (layer, b, 0, 0))
    return pl.pallas_call(
        _sb_attn_past_kernel, grid=(B,),
        in_specs=[new, new, new, past, past], out_specs=new,
        scratch_shapes=[pltpu.VMEM((N_HEADS * Tn, Tp), F32), pltpu.VMEM((N_HEADS * Tn, Tp), F32),
                        pltpu.VMEM((N_HEADS * Tn, Tn), F32), pltpu.VMEM((N_HEADS * Tn, Tn), F32)],
        out_shape=jax.ShapeDtypeStruct((B, Tn, D), BF),
        compiler_params=_cparams("parallel"),
        name="sb_attention_past",
    )(q, kn, vn, cache_k, cache_v)


def _top2_gates(logits):
    lane = lax.broadcasted_iota(jnp.int32, logits.shape, 1)
    logits = jnp.where(lane < N_EXPERTS, logits, -jnp.inf)
    v1 = jnp.max(logits, axis=-1, keepdims=True)
    i1 = jnp.min(jnp.where(logits == v1, lane, LANES), axis=-1, keepdims=True)
    rest = jnp.where(lane == i1, -jnp.inf, logits)
    v2 = jnp.max(rest, axis=-1, keepdims=True)
    i2 = jnp.min(jnp.where(rest == v2, lane, LANES), axis=-1, keepdims=True)
    e2 = jnp.exp(v2 - v1)
    return jnp.where(lane == i1, 1.0 / (1.0 + e2), 0.0) + jnp.where(lane == i2, e2 / (1.0 + e2), 0.0)


def _ffn_kernel(*refs, moe, mixed):
    refs = list(refs)
    x_ref, g_ref, sh_ref, sc_ref, gate_ref = refs[:5]
    del refs[:5]
    if mixed:
        a_ref, wo_ref, mix_gate_ref = refs[:3]
        del refs[:3]
    if moe:
        wr_ref = refs.pop(0)
    wg_ref, wu_ref, wd_ref, o_ref, h_sc, acc_sc = refs[:6]
    f = pl.program_id(2)

    @pl.when(f == 0)
    def _():
        x = x_ref[0]
        if mixed:
            x = x + mix_gate_ref[0] * _dot(a_ref[0], wo_ref[...])
        o_ref[0] = x
        h = _normmod(x, g_ref[...], sh_ref[0], sc_ref[0])
        h_sc[...] = h.astype(BF)
        acc_sc[...] = jnp.zeros_like(acc_sc)
        if moe:
            refs[6][...] = _top2_gates(_dot(h, wr_ref[...], precision=HIGHEST))

    hb = h_sc[...]
    g = _dot(hb, wg_ref[...])
    u = _dot(hb, wu_ref[...])
    a = g * jax.nn.sigmoid(g) * u
    if moe:
        gates = refs[6][...]
        lane = lax.broadcasted_iota(jnp.int32, gates.shape, 1)
        a = a * jnp.sum(jnp.where(lane == f, gates, 0.0), axis=-1, keepdims=True)
    acc_sc[...] += _dot(a.astype(BF), wd_ref[...])

    @pl.when(f == pl.num_programs(2) - 1)
    def _():
        o_ref[0] = o_ref[0] + gate_ref[0] * acc_sc[...]


def _ffn(x, gain, shift, scale, gate, w_gate, w_up, w_down, w_router=None, mixer=None):
    B, T, D = x.shape
    tm = min(T, FFN_ROW_TILE)
    moe = w_router is not None
    mixed = mixer is not None
    row = pl.BlockSpec((1, tm, D), lambda b, i, f: (b, i, 0))
    in_specs = [row, _const_spec((1, D)), _mod_spec(shift, tm), _mod_spec(scale, tm), _mod_spec(gate, tm)]
    args = [x, gain.reshape(1, D), shift, scale, gate]
    if mixed:
        a, w_out, mix_gate = mixer
        in_specs += [pl.BlockSpec((1, tm, a.shape[-1]), lambda b, i, f: (b, i, 0)),
                     _const_spec(w_out.shape), _mod_spec(mix_gate, tm)]
        args += [a, w_out, mix_gate]
    scratch = [pltpu.VMEM((tm, D), BF), pltpu.VMEM((tm, D), F32)]
    if moe:
        nf, _, tf = w_gate.shape
        in_specs += [_const_spec((D, LANES)),
                     pl.BlockSpec((None, D, tf), lambda b, i, f: (f, 0, 0)),
                     pl.BlockSpec((None, D, tf), lambda b, i, f: (f, 0, 0)),
                     pl.BlockSpec((None, tf, D), lambda b, i, f: (f, 0, 0))]
        args.append(jnp.pad(w_router, ((0, 0), (0, LANES - N_EXPERTS))))
        scratch.append(pltpu.VMEM((tm, LANES), F32))
    else:
        tf = FFN_COL_TILE
        nf = w_gate.shape[1] // tf
        in_specs += [pl.BlockSpec((D, tf), lambda b, i, f: (0, f)),
                     pl.BlockSpec((D, tf), lambda b, i, f: (0, f)),
                     pl.BlockSpec((tf, D), lambda b, i, f: (f, 0))]
    args += [w_gate, w_up, w_down]
    return pl.pallas_call(
        functools.partial(_ffn_kernel, moe=moe, mixed=mixed),
        grid=(B, T // tm, nf),
        in_specs=in_specs, out_specs=row, scratch_shapes=scratch,
        out_shape=jax.ShapeDtypeStruct((B, T, D), F32),
        compiler_params=_cparams("parallel", "parallel", "arbitrary"),
        name="moe_ffn" if moe else "dense_ffn",
    )(*args)


def _lru_proj_kernel(x_ref, g_ref, sh_ref, sc_ref, w_ref, gate_ref, xr_ref, tail_ref, *, tail_rows):
    h = _normmod(x_ref[0], g_ref[...], sh_ref[0], sc_ref[0])
    hb = h.astype(BF)
    tm = hb.shape[0]
    for c in range(2 * D_RNN // MXU_N):
        acc = _dot(hb, w_ref[:, c * MXU_N:(c + 1) * MXU_N])
        if c * MXU_N < D_RNN:
            gate_ref[0, :, c * MXU_N:(c + 1) * MXU_N] = jax.nn.gelu(acc).astype(BF)
        else:
            sl = slice(c * MXU_N - D_RNN, (c + 1) * MXU_N - D_RNN)
            xr_ref[0, :, sl] = acc.astype(BF)
            tail_ref[0, :, sl] = acc[tm - tail_rows:, :]


def _lru_proj(x, gain, shift, scale, w_in, all_rows_f32):
    B, T, D = x.shape
    tm = min(T, ROW_TILE)
    tail_rows = tm if all_rows_f32 else SUBLANES
    tail_spec = pl.BlockSpec((1, tail_rows, D_RNN), (lambda b, i: (b, i, 0)) if all_rows_f32
                                       else (lambda b, i: (b, 0, 0)))
    row = pl.BlockSpec((1, tm, D), lambda b, i: (b, i, 0))
    rnn = pl.BlockSpec((1, tm, D_RNN), lambda b, i: (b, i, 0))
    return pl.pallas_call(
        functools.partial(_lru_proj_kernel, tail_rows=tail_rows), grid=(B, T // tm),
        in_specs=[row, _const_spec((1, D)), _mod_spec(shift, tm), _mod_spec(scale, tm),
                  _const_spec((D, 2 * D_RNN))],
        out_specs=[rnn, rnn, tail_spec],
        out_shape=[jax.ShapeDtypeStruct((B, T, D_RNN), BF), jax.ShapeDtypeStruct((B, T, D_RNN), BF),
                   jax.ShapeDtypeStruct((B, T if all_rows_f32 else SUBLANES, D_RNN), F32)],
        compiler_params=_cparams("parallel", "arbitrary"),
        name="lru_proj",
    )(x, gain.reshape(1, D), shift, scale, w_in)


def _lru_core_kernel(gg_ref, xr_ref, x_ref, gate_ref, h0_ref, prev_ref, cw_ref, cb_ref, wri_ref, bri_ref,
                     lam_ref, wo_ref, o_ref, hl_ref, xp_sc, a_sc, b_sc, hs_sc, h_sc):
    i = pl.program_id(1)
    tm = xr_ref.shape[1]

    @pl.when(i == 0)
    def _():
        xp_sc[0:SUBLANES, :] = prev_ref[0]
        h_sc[...] = h0_ref[0]

    xp_sc[SUBLANES:, :] = xr_ref[0].astype(F32)
    xc = cb_ref[...]
    for j in range(CONV_W):
        off = SUBLANES - (CONV_W - 1) + j
        xc = xc + cw_ref[j:j + 1, :] * xp_sc[off:off + tm, :]
    xp_sc[0:SUBLANES, :] = xp_sc[tm:tm + SUBLANES, :]

    xcb = xc.astype(BF)
    neg_decay = -LRU_C * _softplus(-lam_ref[...])
    for s in range(D_RNN // RNN_SUPER):
        sl = slice(s * RNN_SUPER, (s + 1) * RNN_SUPER)
        z = _dot(xcb[:, sl], wri_ref[s])
        r = jax.nn.sigmoid(z[:, :RNN_SUPER] + bri_ref[0:1, sl])
        g = jax.nn.sigmoid(z[:, RNN_SUPER:] + bri_ref[1:2, sl])
        log_a = neg_decay[:, sl] * r
        y = 2.0 * log_a
        a_sc[:, sl] = jnp.exp(log_a)
        b_sc[:, sl] = jnp.sqrt(-jnp.tanh(log_a) * (jnp.exp(y) + 1.0)) * (g * xc[:, sl])

    def step(t, h):
        h = a_sc[pl.ds(t, 1), :] * h + b_sc[pl.ds(t, 1), :]
        hs_sc[pl.ds(t, 1), :] = h
        return h

    h_last = lax.fori_loop(0, tm, step, h_sc[...], unroll=8)
    h_sc[...] = h_last
    hl_ref[0] = h_last
    y = (gg_ref[0].astype(F32) * hs_sc[...]).astype(BF)
    o_ref[0] = x_ref[0] + gate_ref[0] * _dot(y, wo_ref[...])


def _block_diag(w):
    n_super = D_RNN // RNN_SUPER
    per = N_RNN_BLOCKS // n_super
    w = w.reshape(n_super, per, w.shape[1], w.shape[2])
    return jnp.einsum('sbij,bc->sbicj', w, jnp.eye(per, dtype=w.dtype)).reshape(n_super, RNN_SUPER, RNN_SUPER)


def _lru_core(gg, xr, x, gate, h0, prev, conv_w, conv_b, w_r, b_r, w_i, b_i, lam, w_out):
    B, T, D = x.shape
    tm = min(T, ROW_TILE)
    w_ri = jnp.concatenate([_block_diag(w_r), _block_diag(w_i)], axis=-1).astype(BF)
    b_ri = jnp.stack([b_r, b_i])
    row = pl.BlockSpec((1, tm, D), lambda b, i: (b, i, 0))
    rnn = pl.BlockSpec((1, tm, D_RNN), lambda b, i: (b, i, 0))
    n_super = D_RNN // RNN_SUPER
    return pl.pallas_call(
        _lru_core_kernel, grid=(B, T // tm),
        in_specs=[rnn, rnn, row, _mod_spec(gate, tm),
                  pl.BlockSpec((1, 1, D_RNN), lambda b, i: (b, 0, 0)),
                  pl.BlockSpec((1, SUBLANES, D_RNN), lambda b, i: (b, 0, 0)),
                  _const_spec((CONV_W, D_RNN)), _const_spec((1, D_RNN)),
                  _const_spec((n_super, RNN_SUPER, 2 * RNN_SUPER)), _const_spec((2, D_RNN)),
                  _const_spec((1, D_RNN)), _const_spec((D_RNN, D))],
        out_specs=[row, pl.BlockSpec((1, 1, D_RNN), lambda b, i: (b, 0, 0))],
        out_shape=[jax.ShapeDtypeStruct((B, T, D), F32), jax.ShapeDtypeStruct((B, 1, D_RNN), F32)],
        scratch_shapes=[pltpu.VMEM((tm + SUBLANES, D_RNN), F32), pltpu.VMEM((tm, D_RNN), F32),
                        pltpu.VMEM((tm, D_RNN), F32), pltpu.VMEM((tm, D_RNN), F32),
                        pltpu.VMEM((1, D_RNN), F32)],
        compiler_params=_cparams("parallel", "arbitrary"),
        name="lru_core",
    )(gg, xr, x, gate, h0, prev, conv_w, conv_b.reshape(1, D_RNN), w_ri, b_ri, lam.reshape(1, D_RNN), w_out)


def _trunk(x, mod, W, past):
    B, T, D = x.shape
    depth = mod.shape[0]
    has_past = past is not None
    flat = (lambda a: a.reshape(1, B * T, a.shape[-1])) if has_past else (lambda a: a)
    unflat = (lambda a: a.reshape(B, T, a.shape[-1])) if has_past else (lambda a: a)
    n_fox = len(range(0, depth, N_MIXERS))
    n_sb = len(range(2, depth, N_MIXERS))
    fox_kv = sb_kv = None
    fox_logf, lru_new = [], []
    for i in range(depth):
        per_batch = [m[:, None, :] for m in jnp.split(mod[i], 6, axis=-1)]
        if has_past:
            per_row = [jnp.broadcast_to(m, (B, T, D)).reshape(1, B * T, D) for m in per_batch]
        else:
            per_row = per_batch
        shift1, scale1, gate1, shift2, scale2, gate2 = per_row
        kind, j = i % N_MIXERS, i // N_MIXERS
        if kind == 0:
            fox_params = (W['fox_w_f'][j], W['fox_b_f'][j], W['fox_q_norm'][j], W['fox_k_norm'][j])
            q, k32, kbf, v32, vbf, logf = _qkv_proj(flat(x), W['norm_mix'][i], shift1, scale1,
                                                    W['fox_w_qkv'][j], fox_params, n_fox, j, fox_kv)
            fox_kv = (k32, v32)
            q, kbf, vbf, logf = (unflat(a) for a in (q, kbf, vbf, logf))
            fox_logf.append(logf[..., :N_HEADS])
            if has_past:
                cache_k, cache_v, cache_lf = past['fox']
                Tp = cache_k.shape[2]
                lf_all = jnp.concatenate([jnp.swapaxes(cache_lf[j], 1, 2),
                                          jnp.swapaxes(fox_logf[-1], 1, 2)], axis=-1)
                pad = (-lf_all.shape[-1]) % LANES
                cum = _cumsum_flat(jnp.pad(lf_all, ((0, 0), (0, 0), (0, pad))).reshape(B * N_HEADS, -1))
                cum = cum.reshape(B, N_HEADS, -1) * LOG2E
                o = _fox_attention_past(q, kbf, vbf, cache_k.reshape(*cache_k.shape[:3], D),
                                        cache_v.reshape(*cache_v.shape[:3], D), j,
                                        cum[..., :Tp], cum[..., Tp:Tp + T])
            else:
                bound = 1.02 * HEAD_DIM ** 0.5 * jnp.max(jnp.abs(W['fox_q_norm'][j])) \
                    * jnp.max(jnp.abs(W['fox_k_norm'][j]))
                o = _fox_attention(q, kbf, vbf, logf, bound, min(T, FOX_BLOCK))
            mixer = (flat(o), W['fox_w_out'][j], gate1)
        elif kind == 1:
            gg, xr, tail = _lru_proj(flat(x), W['norm_mix'][i], shift1, scale1, W['lru_w_in'][j], has_past)
            gg, xr, tail = unflat(gg), unflat(xr), unflat(tail) if has_past else tail
            if has_past:
                state_h, state_conv = past['lru']
                h0 = state_h[j][:, None, :]
                prev = jnp.pad(state_conv[j], ((0, 0), (SUBLANES - (CONV_W - 1), 0), (0, 0)))
            else:
                h0 = jnp.zeros((B, 1, D_RNN), F32)
                prev = jnp.zeros((B, SUBLANES, D_RNN), F32)
            x, h_last = _lru_core(gg, xr, x, per_batch[2], h0, prev, W['lru_conv_w'][j], W['lru_conv_b'][j],
                                  W['lru_w_r'][j], W['lru_b_r'][j], W['lru_w_i'][j], W['lru_b_i'][j],
                                  W['lru_lambda'][j], W['lru_w_out'][j])
            lru_new.append((h_last[:, 0], tail[:, tail.shape[1] - (CONV_W - 1):]))
            mixer = None
        else:
            q, k32, kbf, v32, vbf = _qkv_proj(flat(x), W['norm_mix'][i], shift1, scale1, W['sb_w_in'][j],
                                              None, n_sb, j, sb_kv)
            sb_kv = (k32, v32)
            q, kbf, vbf = (unflat(a) for a in (q, kbf, vbf))
            if has_past:
                cache_k, cache_v = past['sb']
                o = _sb_attention_past(q, kbf, vbf, cache_k.reshape(*cache_k.shape[:3], D),
                                       cache_v.reshape(*cache_v.shape[:3], D), j)
            else:
                o = _sb_attention(q, kbf, vbf, min(T, SB_BLOCK))
            mixer = (flat(o), W['sb_w_out'][j], gate1)
        if i % 2 == 0:
            x = unflat(_ffn(flat(x), W['norm_ffn'][i], shift2, scale2, gate2,
                            W['ffn_w_gate'][i // 2], W['ffn_w_up'][i // 2], W['ffn_w_down'][i // 2],
                            None, mixer))
        else:
            x = unflat(_ffn(flat(x), W['norm_ffn'][i], shift2, scale2, gate2,
                            W['moe_w_gate'][i // 2], W['moe_w_up'][i // 2], W['moe_w_down'][i // 2],
                            W['moe_w_router'][i // 2], mixer))

    def heads(state):
        return state.reshape(state.shape[0], B, T, N_HEADS, HEAD_DIM)

    fox_state = [heads(fox_kv[0]), heads(fox_kv[1]), jnp.stack(fox_logf)]
    lru_state = [jnp.stack([s[n] for s in lru_new]) for n in range(2)]
    sb_state = [heads(sb_kv[0]), heads(sb_kv[1])]
    return x, fox_state, lru_state, sb_state


MATMUL_WEIGHTS = ('fox_w_qkv', 'fox_w_out', 'lru_w_in', 'lru_w_out', 'sb_w_in', 'sb_w_out',
                  'ffn_w_gate', 'ffn_w_up', 'ffn_w_down', 'moe_w_gate', 'moe_w_up', 'moe_w_down')


def _prepare_weights(W):
    W = dict(W)
    w_in = W.pop('fox_w_in')
    W['fox_w_qkv'] = w_in[:, :, :3 * D_MODEL]
    W['fox_w_f'] = w_in[:, :, 3 * D_MODEL:]
    for name in MATMUL_WEIGHTS:
        W[name] = W[name].astype(BF)
    return W


def _forward(W, x_prompt, c_prompt, x_sample, c_sample, past):
    W = _prepare_weights(W)
    n_prompt, n_sample = c_prompt.shape[0], c_sample.shape[0]
    c_all = jnp.concatenate([c_prompt, c_sample], axis=0)
    c_all = jnp.pad(c_all, ((0, (-c_all.shape[0]) % (2 * SUBLANES)), (0, 0)))
    mod = _ada(c_all, W['w_ada'], W['b_ada'])
    y_p, fox_p, lru_p, sb_p = _trunk(x_prompt, mod[:, :n_prompt], W, None)
    y_s, fox_s, lru_s, sb_s = _trunk(x_sample, mod[:, n_prompt:n_prompt + n_sample], W, past)

    return (y_p, y_s, *fox_p, *fox_s, *lru_p, *lru_s, *sb_p, *sb_s)


def kernel(x_prompt, x_sample, cache_fox_k, cache_fox_v, cache_fox_logf, state_lru_h, state_lru_conv,
           cache_sb_k, cache_sb_v, c_prompt, c_sample, norm_mix, norm_ffn, w_ada, b_ada,
           fox_w_in, fox_b_f, fox_q_norm, fox_k_norm, fox_w_out,
           lru_w_in, lru_conv_w, lru_conv_b, lru_w_r, lru_b_r, lru_w_i, lru_b_i, lru_lambda, lru_w_out,
           sb_w_in, sb_w_out, ffn_w_gate, ffn_w_up, ffn_w_down,
           moe_w_router, moe_w_gate, moe_w_up, moe_w_down):
    W = {
        'norm_mix': norm_mix, 'norm_ffn': norm_ffn, 'w_ada': w_ada, 'b_ada': b_ada,
        'fox_w_in': fox_w_in, 'fox_b_f': fox_b_f, 'fox_q_norm': fox_q_norm,
        'fox_k_norm': fox_k_norm, 'fox_w_out': fox_w_out,
        'lru_w_in': lru_w_in, 'lru_conv_w': lru_conv_w, 'lru_conv_b': lru_conv_b,
        'lru_w_r': lru_w_r, 'lru_b_r': lru_b_r, 'lru_w_i': lru_w_i, 'lru_b_i': lru_b_i,
        'lru_lambda': lru_lambda, 'lru_w_out': lru_w_out,
        'sb_w_in': sb_w_in, 'sb_w_out': sb_w_out,
        'ffn_w_gate': ffn_w_gate, 'ffn_w_up': ffn_w_up, 'ffn_w_down': ffn_w_down,
        'moe_w_router': moe_w_router, 'moe_w_gate': moe_w_gate, 'moe_w_up': moe_w_up,
        'moe_w_down': moe_w_down,
    }
    past = {'fox': (cache_fox_k, cache_fox_v, cache_fox_logf), 'lru': (state_lru_h, state_lru_conv),
            'sb': (cache_sb_k, cache_sb_v)}
    return _forward(W, x_prompt, c_prompt, x_sample, c_sample, past)
```

```python
import functools

import jax
import jax.numpy as jnp
from jax import lax
from jax.experimental import pallas as pl
from jax.experimental.pallas import tpu as pltpu

F32 = jnp.float32
BF = jnp.bfloat16
HIGHEST = lax.Precision.HIGHEST

D_MODEL = 1024
N_HEADS = 8
HEAD_DIM = 128
N_MIXERS = 3
D_RNN = 1280
N_RNN_BLOCKS = 16
RNN_SUPER = 640
CONV_W = 4
LRU_C = 8.0
N_EXPERTS = 8
EPS = 1e-6
ATTN_SCALE = HEAD_DIM ** -0.5
LANES = 128
SUBLANES = 8
MXU_N = 256
EXP_UNDERFLOW = 104.0
EXP2_UNDERFLOW = 150.0
LOG2E = 1.4426950408889634
Q_SCALE = ATTN_SCALE * LOG2E
EXP2_FLUSH = 126.0
FIXED_REF_MAX_BOUND = 24.0
VMEM_LIMIT = 56 * 2 ** 20

FOX_BLOCK = 512
SB_BLOCK = 256
ROW_TILE = 512
FFN_ROW_TILE = 1024
FFN_COL_TILE = 512


def _cparams(*sem):
    return pltpu.CompilerParams(dimension_semantics=sem, vmem_limit_bytes=VMEM_LIMIT)


def _dot(a, b, **kw):
    return jnp.dot(a, b, preferred_element_type=F32, **kw)


def _dot_nt(a, b):
    return lax.dot_general(a, b, (((1,), (1,)), ((), ())), preferred_element_type=F32)


def _normmod(x, gain, shift, scale):
    ms = jnp.mean(x * x, axis=-1, keepdims=True)
    return x * lax.rsqrt(ms + EPS) * gain * (1.0 + scale) + shift


def _split_bf16(x):
    hi = x.astype(BF)
    return hi, (x - hi.astype(F32)).astype(BF)


def _dot_3pass(a, w_hi, w_lo):
    a_hi, a_lo = _split_bf16(a)
    return _dot(a_hi, w_hi) + _dot(a_lo, w_hi) + _dot(a_hi, w_lo)


def _softplus(x):
    return jnp.maximum(x, 0.0) + jnp.log1p(jnp.exp(-jnp.abs(x)))


def _mod_spec(mod, tm):
    d = mod.shape[-1]
    if mod.shape[1] == 1:
        return pl.BlockSpec((1, 1, d), lambda b, i, *_: (b, 0, 0))
    return pl.BlockSpec((1, tm, d), lambda b, i, *_: (b, i, 0))


def _const_spec(shape):
    zeros = (0,) * len(shape)
    return pl.BlockSpec(shape, lambda *_: zeros)


def _ada_kernel(c_ref, w_ref, b_ref, o_ref):
    c = c_ref[...]
    c_act = (c * jax.nn.sigmoid(c)).astype(BF)
    o_ref[0] = _dot(c_act, w_ref[0].astype(BF)) + b_ref[0]


def _ada(c_all, w_ada, b_ada):
    rows, d = c_all.shape
    depth, _, n = w_ada.shape
    tn = 1024
    return pl.pallas_call(
        _ada_kernel,
        grid=(depth, n // tn),
        in_specs=[pl.BlockSpec((rows, d), lambda l, j: (0, 0)),
                  pl.BlockSpec((1, d, tn), lambda l, j: (l, 0, j)),
                  pl.BlockSpec((1, 1, tn), lambda l, j: (l, 0, j))],
        out_specs=pl.BlockSpec((1, rows, tn), lambda l, j: (l, 0, j)),
        out_shape=jax.ShapeDtypeStruct((depth, rows, n), F32),
        compiler_params=_cparams("parallel", "parallel"),
        name="ada_modulation",
    )(c_all, w_ada, b_ada.reshape(depth, 1, n))


def _qkv_kernel(*refs, fox, aliased):
    refs = list(refs)
    x_ref, g_ref, sh_ref, sc_ref, w_ref = refs[:5]
    del refs[:5]
    if fox:
        wf_hi_ref, wf_lo_ref, bf_ref, qg_ref, kg_ref = refs[:5]
        del refs[:5]
    if aliased:
        del refs[:2]
    q_ref, k32_ref, kbf_ref, v32_ref, vbf_ref = refs[:5]
    h = _normmod(x_ref[0], g_ref[...], sh_ref[0], sc_ref[0])
    hb = h.astype(BF)
    tm = hb.shape[0]

    def head_norm(a, gain_ref):
        ms = jnp.mean(a * a, axis=-1, keepdims=True)
        return a * lax.rsqrt(ms + EPS) * gain_ref[...]

    for c in range(3 * D_MODEL // MXU_N):
        acc = _dot(hb, w_ref[:, c * MXU_N:(c + 1) * MXU_N])
        part = (c * MXU_N) // D_MODEL
        for hh in range(MXU_N // HEAD_DIM):
            col = (c * MXU_N) % D_MODEL + hh * HEAD_DIM
            sl = slice(col, col + HEAD_DIM)
            state_rows = pl.ds(col // HEAD_DIM, tm, stride=N_HEADS)
            a = acc[:, hh * HEAD_DIM:(hh + 1) * HEAD_DIM]
            if part == 0:
                if fox:
                    a = head_norm(a, qg_ref)
                q_ref[0, :, sl] = (a * Q_SCALE).astype(BF)
            elif part == 1:
                if fox:
                    a = head_norm(a, kg_ref)
                k32_ref[0, state_rows, :] = a
                kbf_ref[0, :, sl] = a.astype(BF)
            else:
                v32_ref[0, state_rows, :] = a
                vbf_ref[0, :, sl] = a.astype(BF)
    if fox:
        f_logit = _dot_3pass(h, wf_hi_ref[...], wf_lo_ref[...]) + bf_ref[...]
        refs[5][0] = -_softplus(-f_logit)


def _qkv_proj(x, gain, shift, scale, w_qkv, fox_params=None, n_slots=1, slot=0, state=None):
    B, T, D = x.shape
    tm = min(T, ROW_TILE)
    fox = fox_params is not None
    aliased = state is not None
    row = pl.BlockSpec((1, tm, D), lambda b, i: (b, i, 0))
    state_spec = pl.BlockSpec((None, 1, tm * N_HEADS, HEAD_DIM), lambda b, i: (slot, b, i, 0))
    state_shape = jax.ShapeDtypeStruct((n_slots, B, T * N_HEADS, HEAD_DIM), F32)
    in_specs = [row, _const_spec((1, D)), _mod_spec(shift, tm), _mod_spec(scale, tm), _const_spec((D, 3 * D))]
    args = [x, gain.reshape(1, D), shift, scale, w_qkv]
    out_specs = [row, state_spec, row, state_spec, row]
    out_shape = [jax.ShapeDtypeStruct((B, T, D), BF), state_shape, jax.ShapeDtypeStruct((B, T, D), BF),
                 state_shape, jax.ShapeDtypeStruct((B, T, D), BF)]
    if fox:
        w_f, b_f, q_gain, k_gain = fox_params
        in_specs += [_const_spec((D, LANES)), _const_spec((D, LANES)), _const_spec((1, LANES)),
                     _const_spec((1, HEAD_DIM)), _const_spec((1, HEAD_DIM))]
        args += [*_split_bf16(jnp.pad(w_f, ((0, 0), (0, LANES - N_HEADS)))),
                 jnp.pad(b_f, (0, LANES - N_HEADS)).reshape(1, LANES),
                 q_gain.reshape(1, HEAD_DIM), k_gain.reshape(1, HEAD_DIM)]
        out_specs.append(pl.BlockSpec((1, tm, LANES), lambda b, i: (b, i, 0)))
        out_shape.append(jax.ShapeDtypeStruct((B, T, LANES), F32))
    aliases = {}
    if aliased:
        aliases = {len(args): 1, len(args) + 1: 3}
        in_specs += [pl.BlockSpec(memory_space=pl.ANY)] * 2
        args += list(state)
    return pl.pallas_call(
        functools.partial(_qkv_kernel, fox=fox, aliased=aliased),
        grid=(B, T // tm),
        in_specs=in_specs, out_specs=out_specs, out_shape=out_shape,
        input_output_aliases=aliases,
        compiler_params=_cparams("parallel", "parallel"),
        name="fox_qkv_proj" if fox else "sb_qkv_proj",
    )(*args)


def _split3_bf16(x):
    hi = x.astype(BF).astype(F32)
    mid = (x - hi).astype(BF).astype(F32)
    return hi, mid, (x - hi - mid).astype(BF).astype(F32)


def _fox_prep_kernel(bound_ref, lf_ref, qx_ref, kx_ref, cum_ref, bmin_ref, bmax_ref, carry_sc):
    i = pl.program_id(1)
    tm = lf_ref.shape[1]

    @pl.when(i == 0)
    def _():
        carry_sc[...] = jnp.zeros_like(carry_sc)

    r = lax.broadcasted_iota(jnp.int32, (tm, tm), 0)
    c = lax.broadcasted_iota(jnp.int32, (tm, tm), 1)
    before = (c <= r).astype(BF)
    cum = carry_sc[...]
    for piece in _split3_bf16(lf_ref[0]):
        cum = cum + _dot(before, piece.astype(BF))
    carry_sc[...] = cum[tm - 1:tm, :]
    cum_ref[0] = cum
    bmin_ref[0, 0] = jnp.min(cum, axis=0, keepdims=True)
    bmax_ref[0, 0] = jnp.max(cum, axis=0, keepdims=True)
    c2 = cum * LOG2E
    bound2 = bound_ref[0] * LOG2E
    lane = lax.broadcasted_iota(jnp.int32, (tm, LANES), 1)
    for h in range(N_HEADS):
        col = jnp.broadcast_to(c2[:, h:h + 1], (tm, LANES))
        k1, k2, k3 = _split3_bf16(-col)
        q1, q2, q3 = _split3_bf16(col - bound2)
        kx = jnp.where(lane == 0, k1, jnp.where(lane == 1, k2, jnp.where(lane == 2, k3,
                                                                         jnp.where(lane < 6, 1.0, 0.0))))
        qx = jnp.where(lane < 3, 1.0, jnp.where(lane == 3, q1, jnp.where(lane == 4, q2,
                                                                         jnp.where(lane == 5, q3, 0.0))))
        sl = slice(h * LANES, (h + 1) * LANES)
        qx_ref[0, :, sl] = qx.astype(BF)
        kx_ref[0, :, sl] = kx.astype(BF)


def _fox_prep(logf, bound, blk):
    B, T, _ = logf.shape
    nb = T // blk
    row = pl.BlockSpec((1, blk, LANES), lambda b, i: (b, i, 0))
    wide = pl.BlockSpec((1, blk, N_HEADS * LANES), lambda b, i: (b, i, 0))
    stat = pl.BlockSpec((1, 1, 1, LANES), lambda b, i: (b, i, 0, 0))
    return pl.pallas_call(
        _fox_prep_kernel, grid=(B, nb),
        in_specs=[pl.BlockSpec(memory_space=pltpu.SMEM), row],
        out_specs=[wide, wide, row, stat, stat],
        out_shape=[jax.ShapeDtypeStruct((B, T, N_HEADS * LANES), BF)] * 2
        + [jax.ShapeDtypeStruct((B, T, LANES), F32)] + [jax.ShapeDtypeStruct((B, nb, 1, LANES), F32)] * 2,
        scratch_shapes=[pltpu.VMEM((1, LANES), F32)],
        compiler_params=_cparams("parallel", "arbitrary"),
        name="fox_prep",
    )(bound.reshape(1), logf)


def _cumsum_flat_kernel(x_ref, o_ref):
    n = x_ref.shape[-1]
    r = lax.broadcasted_iota(jnp.int32, (n, n), 0)
    c = lax.broadcasted_iota(jnp.int32, (n, n), 1)
    o_ref[...] = _dot(x_ref[...], (r <= c).astype(F32), precision=HIGHEST)


def _cumsum_flat(x):
    return pl.pallas_call(
        _cumsum_flat_kernel, out_shape=jax.ShapeDtypeStruct(x.shape, F32),
        compiler_params=pltpu.CompilerParams(vmem_limit_bytes=VMEM_LIMIT),
        name="cumsum_flat",
    )(x)


def _fox_attn_kernel(js_ref, q_ref, qx_ref, k_ref, kx_ref, v_ref, o_ref, acc_sc, *, blk):
    b, h, i = pl.program_id(0), pl.program_id(1), pl.program_id(2)
    nq = pl.num_programs(2)
    q = jnp.concatenate([q_ref[0], qx_ref[0]], axis=-1)
    acc_sc[...] = jnp.zeros_like(acc_sc)
    lane = lax.broadcasted_iota(jnp.int32, (blk, HEAD_DIM), 1)
    ones_col = (lane == 0).astype(BF)

    def block(j, diagonal):
        start = pl.multiple_of(j * blk, blk)
        kb = jnp.concatenate([k_ref[0, pl.ds(start, blk), :], kx_ref[0, pl.ds(start, blk), :]], axis=-1)
        vb = jnp.concatenate([v_ref[0, pl.ds(start, blk), :], ones_col], axis=-1)
        s = _dot_nt(q, kb)
        if diagonal:
            row = lax.broadcasted_iota(jnp.int32, (blk, blk), 0)
            col = lax.broadcasted_iota(jnp.int32, (blk, blk), 1)
            s = jnp.where(col <= row, s, -jnp.inf)
        acc_sc[...] += _dot(jnp.exp2(s).astype(BF), vb)

    j0 = js_ref[(b * N_HEADS + h) * nq + i]
    n_before = i - j0

    def pair(t, carry):
        block(j0 + 2 * t, False)
        block(j0 + 2 * t + 1, False)
        return carry

    lax.fori_loop(0, n_before // 2, pair, 0)

    @pl.when(n_before % 2 == 1)
    def _():
        block(i - 1, False)
        block(i, True)

    @pl.when(n_before % 2 == 0)
    def _():
        block(i, True)

    acc = acc_sc[...]
    o_ref[0] = (acc[:, :HEAD_DIM] / acc[:, HEAD_DIM:HEAD_DIM + 1]).astype(BF)


def _fox_attn_online_kernel(js_ref, q_ref, k_ref, v_ref, ck_ref, o_ref, m_sc, l_sc, acc_sc, *, blk):
    b, h, i = pl.program_id(0), pl.program_id(1), pl.program_id(2)
    nq = pl.num_programs(2)
    q = q_ref[0]
    m_sc[...] = jnp.full_like(m_sc, -jnp.inf)
    l_sc[...] = jnp.zeros_like(l_sc)
    acc_sc[...] = jnp.zeros_like(acc_sc)

    def block(j, diagonal):
        start = pl.multiple_of(j * blk, blk)
        kb = k_ref[0, pl.ds(start, blk), :]
        vb = v_ref[0, pl.ds(start, blk), :]
        s = _dot_nt(q, kb) - ck_ref[0, j]
        if diagonal:
            row = lax.broadcasted_iota(jnp.int32, (blk, blk), 0)
            col = lax.broadcasted_iota(jnp.int32, (blk, blk), 1)
            s = jnp.where(col <= row, s, -jnp.inf)
        m_prev = m_sc[...]
        m_new = jnp.maximum(m_prev, jnp.max(s, axis=-1, keepdims=True))
        alpha = jnp.exp2(m_prev - m_new)
        p = jnp.exp2(s - m_new)
        l_sc[...] = alpha * l_sc[...] + jnp.sum(p, axis=-1, keepdims=True)
        acc_sc[...] = alpha * acc_sc[...] + _dot(p.astype(BF), vb)
        m_sc[...] = m_new

    def body(j, carry):
        block(j, False)
        return carry

    lax.fori_loop(js_ref[(b * N_HEADS + h) * nq + i], i, body, 0)
    block(i, True)
    o_ref[0] = (acc_sc[...] / l_sc[...]).astype(BF)


def _fox_first_blocks(block_min, block_max, dead_gap):
    nb = block_min.shape[-1]
    j = jnp.arange(nb)
    upto = j[None, :] <= j[:, None]
    prefix_min = jnp.min(jnp.where(upto, block_min[..., None, :], jnp.inf), axis=-1)
    dead = (prefix_min[..., None, :] - block_max[..., :, None]) > dead_gap
    return jnp.sum(dead & (j[None, :] < j[:, None]), axis=-1).astype(jnp.int32)


def _fox_attention(q, kbf, vbf, logf, logit_bound, blk):
    B, T, D = q.shape
    nq = T // blk
    qx, kx, cum, bmin, bmax = _fox_prep(logf, logit_bound, blk)
    heads_first = lambda a: jnp.swapaxes(a[:, :, 0, :N_HEADS], 1, 2)
    use_fixed_reference = logit_bound < FIXED_REF_MAX_BOUND
    dead_gap = jnp.where(use_fixed_reference, (EXP2_FLUSH + 1.0) / LOG2E,
                         2.0 * logit_bound + EXP_UNDERFLOW + 1.0)
    first_blocks = _fox_first_blocks(heads_first(bmin), heads_first(bmax), dead_gap).reshape(-1)
    qspec = pl.BlockSpec((1, blk, HEAD_DIM), lambda b, h, i, js: (b, i, h))
    kvspec = pl.BlockSpec((1, T, HEAD_DIM), lambda b, h, i, js: (b, 0, h))
    out_shape = jax.ShapeDtypeStruct((B, T, D), BF)

    def fixed_reference():
        return pl.pallas_call(
            functools.partial(_fox_attn_kernel, blk=blk),
            grid_spec=pltpu.PrefetchScalarGridSpec(
                num_scalar_prefetch=1, grid=(B, N_HEADS, nq),
                in_specs=[qspec, qspec, kvspec, kvspec, kvspec], out_specs=qspec,
                scratch_shapes=[pltpu.VMEM((blk, 2 * HEAD_DIM), F32)]),
            out_shape=out_shape,
            compiler_params=_cparams("parallel", "parallel", "arbitrary"),
            name="fox_attention",
        )(first_blocks, q, qx, kbf, kx, vbf)

    def online():
        ck = jnp.swapaxes(cum[:, :, :N_HEADS], 1, 2) * LOG2E
        ckspec = pl.BlockSpec((1, nq, 1, blk), lambda b, h, i, js: (b * N_HEADS + h, 0, 0, 0))
        return pl.pallas_call(
            functools.partial(_fox_attn_online_kernel, blk=blk),
            grid_spec=pltpu.PrefetchScalarGridSpec(
                num_scalar_prefetch=1, grid=(B, N_HEADS, nq),
                in_specs=[qspec, kvspec, kvspec, ckspec], out_specs=qspec,
                scratch_shapes=[pltpu.VMEM((blk, 1), F32), pltpu.VMEM((blk, 1), F32),
                                pltpu.VMEM((blk, HEAD_DIM), F32)]),
            out_shape=out_shape,
            compiler_params=_cparams("parallel", "parallel", "arbitrary"),
            name="fox_attention_online",
        )(first_blocks, q, kbf, vbf, ck.reshape(B * N_HEADS, nq, 1, blk))

    return lax.cond(use_fixed_reference, fixed_reference, online)


def _fox_attn_past_kernel(q_ref, kn_ref, vn_ref, kp_ref, vp_ref, cp_ref, cn_ref, o_ref):
    tn = q_ref.shape[1]
    row = lax.broadcasted_iota(jnp.int32, (tn, tn), 0)
    col = lax.broadcasted_iota(jnp.int32, (tn, tn), 1)
    for h in range(N_HEADS):
        sl = slice(h * HEAD_DIM, (h + 1) * HEAD_DIM)
        q = q_ref[0, :, sl]
        sp = _dot_nt(q, kp_ref[0, :, sl].astype(BF)) - cp_ref[0, h:h + 1, :]
        sn = _dot_nt(q, kn_ref[0, :, sl]) - cn_ref[0, h:h + 1, :]
        sn = jnp.where(col <= row, sn, -jnp.inf)
        m = jnp.maximum(jnp.max(sp, axis=-1, keepdims=True), jnp.max(sn, axis=-1, keepdims=True))
        pp = jnp.exp2(sp - m)
        pn = jnp.exp2(sn - m)
        l = jnp.sum(pp, axis=-1, keepdims=True) + jnp.sum(pn, axis=-1, keepdims=True)
        o = _dot(pp.astype(BF), vp_ref[0, :, sl].astype(BF)) + _dot(pn.astype(BF), vn_ref[0, :, sl])
        o_ref[0, :, sl] = (o / l).astype(BF)


def _fox_attention_past(q, kn, vn, cache_k, cache_v, layer, cum_past, cum_new):
    B, Tn, D = q.shape
    Tp = cache_k.shape[2]
    new = pl.BlockSpec((1, Tn, D), lambda b: (b, 0, 0))
    past = pl.BlockSpec((None, 1, Tp, D), lambda b: (layer, b, 0, 0))
    return pl.pallas_call(
        _fox_attn_past_kernel, grid=(B,),
        in_specs=[new, new, new, past, past,
                  pl.BlockSpec((1, N_HEADS, Tp), lambda b: (b, 0, 0)),
                  pl.BlockSpec((1, N_HEADS, Tn), lambda b: (b, 0, 0))],
        out_specs=new, out_shape=jax.ShapeDtypeStruct((B, Tn, D), BF),
        compiler_params=_cparams("parallel"),
        name="fox_attention_past",
    )(q, kn, vn, cache_k, cache_v, cum_past, cum_new)


def _neg_log2_rest(z2):
    zc = jnp.minimum(z2, 30.0)
    return jnp.log(1.0 + jnp.exp2(zc)) * LOG2E + (z2 - zc)


class _SbBlock:
    def __init__(self, blk):
        self.row = lax.broadcasted_iota(jnp.int32, (blk, blk), 0)
        self.col = lax.broadcasted_iota(jnp.int32, (blk, blk), 1)
        suffix = jnp.where(self.row >= self.col, -1.0, 0.0).astype(BF)
        self.suffix2 = jnp.concatenate([suffix, suffix], axis=0)

    def scores(self, q, kb, diagonal):
        z2 = _dot_nt(q, kb)
        nlr = _neg_log2_rest(z2)
        if diagonal:
            nlr = jnp.where(self.col < self.row, nlr, 0.0)
        hi, lo = _split_bf16(nlr)
        tail = _dot(jnp.concatenate([hi, lo], axis=-1), self.suffix2)
        return z2 + tail, tail[:, 0:1]

    def weighted(self, e, later, vb, diagonal):
        a = jnp.exp2(e + later)
        if diagonal:
            a = jnp.where(self.col < self.row, a, 0.0)
        return _dot(a.astype(BF), vb)


SB_HEADS_PER_STEP = 2
SB_FIRST_BLOCKS = 3


def _sb_attn_kernel(q_ref, k_ref, v_ref, o_ref, later_sc, acc_sc, *, blk):
    i = pl.program_id(2)
    tile = _SbBlock(blk)

    def head_cols(hh):
        return slice(hh * HEAD_DIM, (hh + 1) * HEAD_DIM)

    def kv(ref, j, hh):
        return ref[0, pl.ds(pl.multiple_of(j * blk, blk), blk), head_cols(hh)]

    def first_chunk(n_blocks):
        for hh in range(SB_HEADS_PER_STEP):
            q = q_ref[0, :, head_cols(hh)]
            later, acc = 0.0, None
            for back in range(n_blocks):
                e, total = tile.scores(q, kv(k_ref, i - back, hh), back == 0)
                part = tile.weighted(e, later, kv(v_ref, i - back, hh), back == 0)
                acc = part if acc is None else acc + part
                later = later + total
            acc_sc[hh] = acc
            later_sc[hh] = later

    for n_blocks in range(1, SB_FIRST_BLOCKS + 1):
        last = n_blocks == SB_FIRST_BLOCKS

        @pl.when((i >= n_blocks - 1) if last else (i == n_blocks - 1))
        def _(n_blocks=n_blocks):
            first_chunk(n_blocks)

    for hh in range(SB_HEADS_PER_STEP):
        q = q_ref[0, :, head_cols(hh)]

        def cond(carry):
            j, live = carry
            return jnp.logical_and(j >= 0, live > -EXP2_UNDERFLOW)

        def body(carry, hh=hh, q=q):
            j, _ = carry
            e, total = tile.scores(q, kv(k_ref, j, hh), False)
            later = later_sc[hh]
            acc_sc[hh] += tile.weighted(e, later, kv(v_ref, j, hh), False)
            later_sc[hh] = later + total
            return j - 1, jnp.max(later_sc[hh])

        lax.while_loop(cond, body, (i - SB_FIRST_BLOCKS, jnp.max(later_sc[hh])))
        o_ref[0, :, head_cols(hh)] = acc_sc[hh].astype(BF)


def _sb_attention(q, kbf, vbf, blk):
    B, T, D = q.shape
    width = SB_HEADS_PER_STEP * HEAD_DIM
    qspec = pl.BlockSpec((1, blk, width), lambda b, h, i: (b, i, h))
    kvspec = pl.BlockSpec((1, T, width), lambda b, h, i: (b, 0, h))
    return pl.pallas_call(
        functools.partial(_sb_attn_kernel, blk=blk),
        grid=(B, N_HEADS // SB_HEADS_PER_STEP, T // blk),
        in_specs=[qspec, kvspec, kvspec], out_specs=qspec,
        scratch_shapes=[pltpu.VMEM((SB_HEADS_PER_STEP, blk, 1), F32),
                        pltpu.VMEM((SB_HEADS_PER_STEP, blk, HEAD_DIM), F32)],
        out_shape=jax.ShapeDtypeStruct((B, T, D), BF),
        compiler_params=_cparams("parallel", "parallel", "arbitrary"),
        name="sb_attention",
    )(q, kbf, vbf)


def _sb_attn_past_kernel(q_ref, kn_ref, vn_ref, kp_ref, vp_ref, o_ref, zp_sc, lrp_sc, zn_sc, lrn_sc):
    tn = q_ref.shape[1]
    tp = kp_ref.shape[1]
    row = lax.broadcasted_iota(jnp.int32, (tn, tn), 0)
    col = lax.broadcasted_iota(jnp.int32, (tn, tn), 1)
    for h in range(N_HEADS):
        sl = slice(h * HEAD_DIM, (h + 1) * HEAD_DIM)
        rows = slice(h * tn, (h + 1) * tn)
        q = q_ref[0, :, sl]
        zp = _dot_nt(q, kp_ref[0, :, sl].astype(BF))
        zn = _dot_nt(q, kn_ref[0, :, sl])
        zp_sc[rows, :] = zp
        lrp_sc[rows, :] = -_neg_log2_rest(zp)
        zn_sc[rows, :] = zn
        lrn_sc[rows, :] = jnp.where(col < row, -_neg_log2_rest(zn), 0.0)
    rp = lax.broadcasted_iota(jnp.int32, (tp, tp), 0)
    cp = lax.broadcasted_iota(jnp.int32, (tp, tp), 1)
    lrp = lrp_sc[...]
    lrn = lrn_sc[...]
    hi, lo = _split_bf16(lrp)
    later_p = (rp > cp).astype(BF)
    between_p = _dot(hi, later_p) + _dot(lo, later_p) + jnp.sum(lrn, axis=-1, keepdims=True)
    hi, lo = _split_bf16(lrn)
    later_n = (row > col).astype(BF)
    between_n = _dot(hi, later_n) + _dot(lo, later_n)
    ap = jnp.exp2(zp_sc[...] + lrp + between_p).astype(BF)
    an = jnp.exp2(zn_sc[...] + lrn + between_n)
    for h in range(N_HEADS):
        sl = slice(h * HEAD_DIM, (h + 1) * HEAD_DIM)
        rows = slice(h * tn, (h + 1) * tn)
        a_new = jnp.where(col < row, an[rows, :], 0.0).astype(BF)
        o = _dot(ap[rows, :], vp_ref[0, :, sl].astype(BF)) + _dot(a_new, vn_ref[0, :, sl])
        o_ref[0, :, sl] = o.astype(BF)


def _sb_attention_past(q, kn, vn, cache_k, cache_v, layer):
    B, Tn, D = q.shape
    Tp = cache_k.shape[2]
    new = pl.BlockSpec((1, Tn, D), lambda b: (b, 0, 0))
    past = pl.BlockSpec((None, 1, Tp, D), lambda b: (layer, b, 0, 0))
    return pl.pallas_call(
        _sb_attn_past_kernel, grid=(B,),
        in_specs=[new, new, new, past, past], out_specs=new,
        scratch_shapes=[pltpu.VMEM((N_HEADS * Tn, Tp), F32), pltpu.VMEM((N_HEADS * Tn, Tp), F32),
                        pltpu.VMEM((N_HEADS * Tn, Tn), F32), pltpu.VMEM((N_HEADS * Tn, Tn), F32)],
        out_shape=jax.ShapeDtypeStruct((B, Tn, D), BF),
        compiler_params=_cparams("parallel"),
        name="sb_attention_past",
    )(q, kn, vn, cache_k, cache_v)


def _top2_gates(logits):
    lane = lax.broadcasted_iota(jnp.int32, logits.shape, 1)
    logits = jnp.where(lane < N_EXPERTS, logits, -jnp.inf)
    v1 = jnp.max(logits, axis=-1, keepdims=True)
    i1 = jnp.min(jnp.where(logits == v1, lane, LANES), axis=-1, keepdims=True)
    rest = jnp.where(lane == i1, -jnp.inf, logits)
    v2 = jnp.max(rest, axis=-1, keepdims=True)
    i2 = jnp.min(jnp.where(rest == v2, lane, LANES), axis=-1, keepdims=True)
    e2 = jnp.exp(v2 - v1)
    return jnp.where(lane == i1, 1.0 / (1.0 + e2), 0.0) + jnp.where(lane == i2, e2 / (1.0 + e2), 0.0)


def _ffn_kernel(*refs, moe, mixed):
    refs = list(refs)
    x_ref, g_ref, sh_ref, sc_ref, gate_ref = refs[:5]
    del refs[:5]
    if mixed:
        a_ref, wo_ref, mix_gate_ref = refs[:3]
        del refs[:3]
    if moe:
        wr_hi_ref, wr_lo_ref = refs[:2]
        del refs[:2]
    wg_ref, wu_ref, wd_ref, o_ref, h_sc, acc_sc = refs[:6]
    f = pl.program_id(2)

    @pl.when(f == 0)
    def _():
        x = x_ref[0]
        if mixed:
            x = x + mix_gate_ref[0] * _dot(a_ref[0], wo_ref[...])
        o_ref[0] = x
        h = _normmod(x, g_ref[...], sh_ref[0], sc_ref[0])
        h_sc[...] = h.astype(BF)
        acc_sc[...] = jnp.zeros_like(acc_sc)
        if moe:
            refs[6][...] = _top2_gates(_dot_3pass(h, wr_hi_ref[...], wr_lo_ref[...]))

    hb = h_sc[...]
    g = _dot(hb, wg_ref[...])
    u = _dot(hb, wu_ref[...])
    a = g * jax.nn.sigmoid(g) * u
    if moe:
        gates = refs[6][...]
        lane = lax.broadcasted_iota(jnp.int32, gates.shape, 1)
        a = a * jnp.sum(jnp.where(lane == f, gates, 0.0), axis=-1, keepdims=True)
    acc_sc[...] += _dot(a.astype(BF), wd_ref[...])

    @pl.when(f == pl.num_programs(2) - 1)
    def _():
        o_ref[0] = o_ref[0] + gate_ref[0] * acc_sc[...]


def _ffn(x, gain, shift, scale, gate, w_gate, w_up, w_down, w_router=None, mixer=None):
    B, T, D = x.shape
    tm = min(T, FFN_ROW_TILE)
    moe = w_router is not None
    mixed = mixer is not None
    row = pl.BlockSpec((1, tm, D), lambda b, i, f: (b, i, 0))
    in_specs = [row, _const_spec((1, D)), _mod_spec(shift, tm), _mod_spec(scale, tm), _mod_spec(gate, tm)]
    args = [x, gain.reshape(1, D), shift, scale, gate]
    if mixed:
        a, w_out, mix_gate = mixer
        in_specs += [pl.BlockSpec((1, tm, a.shape[-1]), lambda b, i, f: (b, i, 0)),
                     _const_spec(w_out.shape), _mod_spec(mix_gate, tm)]
        args += [a, w_out, mix_gate]
    scratch = [pltpu.VMEM((tm, D), BF), pltpu.VMEM((tm, D), F32)]
    if moe:
        nf, _, tf = w_gate.shape
        in_specs += [_const_spec((D, LANES)), _const_spec((D, LANES)),
                     pl.BlockSpec((None, D, tf), lambda b, i, f: (f, 0, 0)),
                     pl.BlockSpec((None, D, tf), lambda b, i, f: (f, 0, 0)),
                     pl.BlockSpec((None, tf, D), lambda b, i, f: (f, 0, 0))]
        args += list(_split_bf16(jnp.pad(w_router, ((0, 0), (0, LANES - N_EXPERTS)))))
        scratch.append(pltpu.VMEM((tm, LANES), F32))
    else:
        tf = FFN_COL_TILE
        nf = w_gate.shape[1] // tf
        in_specs += [pl.BlockSpec((D, tf), lambda b, i, f: (0, f)),
                     pl.BlockSpec((D, tf), lambda b, i, f: (0, f)),
                     pl.BlockSpec((tf, D), lambda b, i, f: (f, 0))]
    args += [w_gate, w_up, w_down]
    return pl.pallas_call(
        functools.partial(_ffn_kernel, moe=moe, mixed=mixed),
        grid=(B, T // tm, nf),
        in_specs=in_specs, out_specs=row, scratch_shapes=scratch,
        out_shape=jax.ShapeDtypeStruct((B, T, D), F32),
        compiler_params=_cparams("parallel", "parallel", "arbitrary"),
        name="moe_ffn" if moe else "dense_ffn",
    )(*args)


def _lru_proj_kernel(x_ref, g_ref, sh_ref, sc_ref, w_ref, gate_ref, xr_ref, tail_ref, *, tail_rows):
    h = _normmod(x_ref[0], g_ref[...], sh_ref[0], sc_ref[0])
    hb = h.astype(BF)
    tm = hb.shape[0]
    for c in range(2 * D_RNN // MXU_N):
        acc = _dot(hb, w_ref[:, c * MXU_N:(c + 1) * MXU_N])
        if c * MXU_N < D_RNN:
            gate_ref[0, :, c * MXU_N:(c + 1) * MXU_N] = jax.nn.gelu(acc).astype(BF)
        else:
            sl = slice(c * MXU_N - D_RNN, (c + 1) * MXU_N - D_RNN)
            xr_ref[0, :, sl] = acc.astype(BF)
            tail_ref[0, :, sl] = acc[tm - tail_rows:, :]


def _lru_proj(x, gain, shift, scale, w_in, all_rows_f32):
    B, T, D = x.shape
    tm = min(T, ROW_TILE)
    tail_rows = tm if all_rows_f32 else SUBLANES
    tail_spec = pl.BlockSpec((1, tail_rows, D_RNN), (lambda b, i: (b, i, 0)) if all_rows_f32
                                       else (lambda b, i: (b, 0, 0)))
    row = pl.BlockSpec((1, tm, D), lambda b, i: (b, i, 0))
    rnn = pl.BlockSpec((1, tm, D_RNN), lambda b, i: (b, i, 0))
    return pl.pallas_call(
        functools.partial(_lru_proj_kernel, tail_rows=tail_rows), grid=(B, T // tm),
        in_specs=[row, _const_spec((1, D)), _mod_spec(shift, tm), _mod_spec(scale, tm),
                  _const_spec((D, 2 * D_RNN))],
        out_specs=[rnn, rnn, tail_spec],
        out_shape=[jax.ShapeDtypeStruct((B, T, D_RNN), BF), jax.ShapeDtypeStruct((B, T, D_RNN), BF),
                   jax.ShapeDtypeStruct((B, T if all_rows_f32 else SUBLANES, D_RNN), F32)],
        compiler_params=_cparams("parallel", "arbitrary"),
        name="lru_proj",
    )(x, gain.reshape(1, D), shift, scale, w_in)


def _lru_core_kernel(gg_ref, xr_ref, x_ref, gate_ref, h0_ref, prev_ref, cw_ref, cb_ref, wri_ref, bri_ref,
                     lam_ref, wo_ref, o_ref, hl_ref, xp_sc, a_sc, b_sc, hs_sc, h_sc):
    i = pl.program_id(1)
    tm = xr_ref.shape[1]

    @pl.when(i == 0)
    def _():
        xp_sc[0:SUBLANES, :] = prev_ref[0]
        h_sc[...] = h0_ref[0]

    xp_sc[SUBLANES:, :] = xr_ref[0].astype(F32)
    xc = cb_ref[...]
    for j in range(CONV_W):
        off = SUBLANES - (CONV_W - 1) + j
        xc = xc + cw_ref[j:j + 1, :] * xp_sc[off:off + tm, :]
    xp_sc[0:SUBLANES, :] = xp_sc[tm:tm + SUBLANES, :]

    xcb = xc.astype(BF)
    neg_decay = -LRU_C * _softplus(-lam_ref[...])
    for s in range(D_RNN // RNN_SUPER):
        sl = slice(s * RNN_SUPER, (s + 1) * RNN_SUPER)
        z = _dot(xcb[:, sl], wri_ref[s])
        r = jax.nn.sigmoid(z[:, :RNN_SUPER] + bri_ref[0:1, sl])
        g = jax.nn.sigmoid(z[:, RNN_SUPER:] + bri_ref[1:2, sl])
        log_a = neg_decay[:, sl] * r
        y = 2.0 * log_a
        a_sc[:, sl] = jnp.exp(log_a)
        b_sc[:, sl] = jnp.sqrt(-jnp.tanh(log_a) * (jnp.exp(y) + 1.0)) * (g * xc[:, sl])

    def step(t, h):
        h = a_sc[pl.ds(t, 1), :] * h + b_sc[pl.ds(t, 1), :]
        hs_sc[pl.ds(t, 1), :] = h
        return h

    h_last = lax.fori_loop(0, tm, step, h_sc[...], unroll=8)
    h_sc[...] = h_last
    hl_ref[0] = h_last
    y = (gg_ref[0].astype(F32) * hs_sc[...]).astype(BF)
    o_ref[0] = x_ref[0] + gate_ref[0] * _dot(y, wo_ref[...])


def _block_diag(w):
    n_super = D_RNN // RNN_SUPER
    per = N_RNN_BLOCKS // n_super
    w = w.reshape(n_super, per, w.shape[1], w.shape[2])
    return jnp.einsum('sbij,bc->sbicj', w, jnp.eye(per, dtype=w.dtype)).reshape(n_super, RNN_SUPER, RNN_SUPER)


def _lru_core(gg, xr, x, gate, h0, prev, conv_w, conv_b, w_r, b_r, w_i, b_i, lam, w_out):
    B, T, D = x.shape
    tm = min(T, ROW_TILE)
    w_ri = jnp.concatenate([_block_diag(w_r), _block_diag(w_i)], axis=-1).astype(BF)
    b_ri = jnp.stack([b_r, b_i])
    row = pl.BlockSpec((1, tm, D), lambda b, i: (b, i, 0))
    rnn = pl.BlockSpec((1, tm, D_RNN), lambda b, i: (b, i, 0))
    n_super = D_RNN // RNN_SUPER
    return pl.pallas_call(
        _lru_core_kernel, grid=(B, T // tm),
        in_specs=[rnn, rnn, row, _mod_spec(gate, tm),
                  pl.BlockSpec((1, 1, D_RNN), lambda b, i: (b, 0, 0)),
                  pl.BlockSpec((1, SUBLANES, D_RNN), lambda b, i: (b, 0, 0)),
                  _const_spec((CONV_W, D_RNN)), _const_spec((1, D_RNN)),
                  _const_spec((n_super, RNN_SUPER, 2 * RNN_SUPER)), _const_spec((2, D_RNN)),
                  _const_spec((1, D_RNN)), _const_spec((D_RNN, D))],
        out_specs=[row, pl.BlockSpec((1, 1, D_RNN), lambda b, i: (b, 0, 0))],
        out_shape=[jax.ShapeDtypeStruct((B, T, D), F32), jax.ShapeDtypeStruct((B, 1, D_RNN), F32)],
        scratch_shapes=[pltpu.VMEM((tm + SUBLANES, D_RNN), F32), pltpu.VMEM((tm, D_RNN), F32),
                        pltpu.VMEM((tm, D_RNN), F32), pltpu.VMEM((tm, D_RNN), F32),
                        pltpu.VMEM((1, D_RNN), F32)],
        compiler_params=_cparams("parallel", "arbitrary"),
        name="lru_core",
    )(gg, xr, x, gate, h0, prev, conv_w, conv_b.reshape(1, D_RNN), w_ri, b_ri, lam.reshape(1, D_RNN), w_out)


def _trunk(x, mod, W, past):
    B, T, D = x.shape
    depth = mod.shape[0]
    has_past = past is not None
    flat = (lambda a: a.reshape(1, B * T, a.shape[-1])) if has_past else (lambda a: a)
    unflat = (lambda a: a.reshape(B, T, a.shape[-1])) if has_past else (lambda a: a)
    n_fox = len(range(0, depth, N_MIXERS))
    n_sb = len(range(2, depth, N_MIXERS))
    fox_kv = sb_kv = None
    fox_logf, lru_new = [], []
    for i in range(depth):
        per_batch = [m[:, None, :] for m in jnp.split(mod[i], 6, axis=-1)]
        if has_past:
            per_row = [jnp.broadcast_to(m, (B, T, D)).reshape(1, B * T, D) for m in per_batch]
        else:
            per_row = per_batch
        shift1, scale1, gate1, shift2, scale2, gate2 = per_row
        kind, j = i % N_MIXERS, i // N_MIXERS
        if kind == 0:
            fox_params = (W['fox_w_f'][j], W['fox_b_f'][j], W['fox_q_norm'][j], W['fox_k_norm'][j])
            q, k32, kbf, v32, vbf, logf = _qkv_proj(flat(x), W['norm_mix'][i], shift1, scale1,
                                                    W['fox_w_qkv'][j], fox_params, n_fox, j, fox_kv)
            fox_kv = (k32, v32)
            q, kbf, vbf, logf = (unflat(a) for a in (q, kbf, vbf, logf))
            fox_logf.append(logf[..., :N_HEADS])
            if has_past:
                cache_k, cache_v, cache_lf = past['fox']
                Tp = cache_k.shape[2]
                lf_all = jnp.concatenate([jnp.swapaxes(cache_lf[j], 1, 2),
                                          jnp.swapaxes(fox_logf[-1], 1, 2)], axis=-1)
                pad = (-lf_all.shape[-1]) % LANES
                cum = _cumsum_flat(jnp.pad(lf_all, ((0, 0), (0, 0), (0, pad))).reshape(B * N_HEADS, -1))
                cum = cum.reshape(B, N_HEADS, -1) * LOG2E
                o = _fox_attention_past(q, kbf, vbf, cache_k.reshape(*cache_k.shape[:3], D),
                                        cache_v.reshape(*cache_v.shape[:3], D), j,
                                        cum[..., :Tp], cum[..., Tp:Tp + T])
            else:
                bound = 1.02 * HEAD_DIM ** 0.5 * jnp.max(jnp.abs(W['fox_q_norm'][j])) \
                    * jnp.max(jnp.abs(W['fox_k_norm'][j]))
                o = _fox_attention(q, kbf, vbf, logf, bound, min(T, FOX_BLOCK))
            mixer = (flat(o), W['fox_w_out'][j], gate1)
        elif kind == 1:
            gg, xr, tail = _lru_proj(flat(x), W['norm_mix'][i], shift1, scale1, W['lru_w_in'][j], has_past)
            gg, xr, tail = unflat(gg), unflat(xr), unflat(tail) if has_past else tail
            if has_past:
                state_h, state_conv = past['lru']
                h0 = state_h[j][:, None, :]
                prev = jnp.pad(state_conv[j], ((0, 0), (SUBLANES - (CONV_W - 1), 0), (0, 0)))
            else:
                h0 = jnp.zeros((B, 1, D_RNN), F32)
                prev = jnp.zeros((B, SUBLANES, D_RNN), F32)
            x, h_last = _lru_core(gg, xr, x, per_batch[2], h0, prev, W['lru_conv_w'][j], W['lru_conv_b'][j],
                                  W['lru_w_r'][j], W['lru_b_r'][j], W['lru_w_i'][j], W['lru_b_i'][j],
                                  W['lru_lambda'][j], W['lru_w_out'][j])
            lru_new.append((h_last[:, 0], tail[:, tail.shape[1] - (CONV_W - 1):]))
            mixer = None
        else:
            q, k32, kbf, v32, vbf = _qkv_proj(flat(x), W['norm_mix'][i], shift1, scale1, W['sb_w_in'][j],
                                              None, n_sb, j, sb_kv)
            sb_kv = (k32, v32)
            q, kbf, vbf = (unflat(a) for a in (q, kbf, vbf))
            if has_past:
                cache_k, cache_v = past['sb']
                o = _sb_attention_past(q, kbf, vbf, cache_k.reshape(*cache_k.shape[:3], D),
                                       cache_v.reshape(*cache_v.shape[:3], D), j)
            else:
                o = _sb_attention(q, kbf, vbf, min(T, SB_BLOCK))
            mixer = (flat(o), W['sb_w_out'][j], gate1)
        if i % 2 == 0:
            x = unflat(_ffn(flat(x), W['norm_ffn'][i], shift2, scale2, gate2,
                            W['ffn_w_gate'][i // 2], W['ffn_w_up'][i // 2], W['ffn_w_down'][i // 2],
                            None, mixer))
        else:
            x = unflat(_ffn(flat(x), W['norm_ffn'][i], shift2, scale2, gate2,
                            W['moe_w_gate'][i // 2], W['moe_w_up'][i // 2], W['moe_w_down'][i // 2],
                            W['moe_w_router'][i // 2], mixer))

    def heads(state):
        return state.reshape(state.shape[0], B, T, N_HEADS, HEAD_DIM)

    fox_state = [heads(fox_kv[0]), heads(fox_kv[1]), jnp.stack(fox_logf)]
    lru_state = [jnp.stack([s[n] for s in lru_new]) for n in range(2)]
    sb_state = [heads(sb_kv[0]), heads(sb_kv[1])]
    return x, fox_state, lru_state, sb_state


MATMUL_WEIGHTS = ('fox_w_qkv', 'fox_w_out', 'lru_w_in', 'lru_w_out', 'sb_w_in', 'sb_w_out',
                  'ffn_w_gate', 'ffn_w_up', 'ffn_w_down', 'moe_w_gate', 'moe_w_up', 'moe_w_down')


def _prepare_weights(W):
    W = dict(W)
    w_in = W.pop('fox_w_in')
    W['fox_w_qkv'] = w_in[:, :, :3 * D_MODEL]
    W['fox_w_f'] = w_in[:, :, 3 * D_MODEL:]
    for name in MATMUL_WEIGHTS:
        W[name] = W[name].astype(BF)
    return W


def _forward(W, x_prompt, c_prompt, x_sample, c_sample, past):
    W = _prepare_weights(W)
    n_prompt, n_sample = c_prompt.shape[0], c_sample.shape[0]
    c_all = jnp.concatenate([c_prompt, c_sample], axis=0)
    c_all = jnp.pad(c_all, ((0, (-c_all.shape[0]) % (2 * SUBLANES)), (0, 0)))
    mod = _ada(c_all, W['w_ada'], W['b_ada'])
    y_p, fox_p, lru_p, sb_p = _trunk(x_prompt, mod[:, :n_prompt], W, None)
    y_s, fox_s, lru_s, sb_s = _trunk(x_sample, mod[:, n_prompt:n_prompt + n_sample], W, past)

    return (y_p, y_s, *fox_p, *fox_s, *lru_p, *lru_s, *sb_p, *sb_s)


def kernel(x_prompt, x_sample, cache_fox_k, cache_fox_v, cache_fox_logf, state_lru_h, state_lru_conv,
           cache_sb_k, cache_sb_v, c_prompt, c_sample, norm_mix, norm_ffn, w_ada, b_ada,
           fox_w_in, fox_b_f, fox_q_norm, fox_k_norm, fox_w_out,
           lru_w_in, lru_conv_w, lru_conv_b, lru_w_r, lru_b_r, lru_w_i, lru_b_i, lru_lambda, lru_w_out,
           sb_w_in, sb_w_out, ffn_w_gate, ffn_w_up, ffn_w_down,
           moe_w_router, moe_w_gate, moe_w_up, moe_w_down):
    W = {
        'norm_mix': norm_mix, 'norm_ffn': norm_ffn, 'w_ada': w_ada, 'b_ada': b_ada,
        'fox_w_in': fox_w_in, 'fox_b_f': fox_b_f, 'fox_q_norm': fox_q_norm,
        'fox_k_norm': fox_k_norm, 'fox_w_out': fox_w_out,
        'lru_w_in': lru_w_in, 'lru_conv_w': lru_conv_w, 'lru_conv_b': lru_conv_b,
        'lru_w_r': lru_w_r, 'lru_b_r': lru_b_r, 'lru_w_i': lru_w_i, 'lru_b_i': lru_b_i,
        'lru_lambda': lru_lambda, 'lru_w_out': lru_w_out,
        'sb_w_in': sb_w_in, 'sb_w_out': sb_w_out,
        'ffn_w_gate': ffn_w_gate, 'ffn_w_up': ffn_w_up, 'ffn_w_down': ffn_w_down,
        'moe_w_router': moe_w_router, 'moe_w_gate': moe_w_gate, 'moe_w_up': moe_w_up,
        'moe_w_down': moe_w_down,
    }
    past = {'fox': (cache_fox_k, cache_fox_v, cache_fox_logf), 'lru': (state_lru_h, state_lru_conv),
            'sb': (cache_sb_k, cache_sb_v)}
    return _forward(W, x_prompt, c_prompt, x_sample, c_sample, past)
```

```python
import functools

import jax
import jax.numpy as jnp
from jax import lax
from jax.experimental import pallas as pl
from jax.experimental.pallas import tpu as pltpu

F32 = jnp.float32
BF = jnp.bfloat16
HIGHEST = lax.Precision.HIGHEST

D_MODEL = 1024
N_HEADS = 8
HEAD_DIM = 128
N_MIXERS = 3
D_RNN = 1280
N_RNN_BLOCKS = 16
RNN_SUPER = 640
CONV_W = 4
LRU_C = 8.0
N_EXPERTS = 8
EPS = 1e-6
ATTN_SCALE = HEAD_DIM ** -0.5
LANES = 128
SUBLANES = 8
MXU_N = 256
EXP_UNDERFLOW = 104.0
EXP2_UNDERFLOW = 150.0
LOG2E = 1.4426950408889634
Q_SCALE = ATTN_SCALE * LOG2E
EXP2_FLUSH = 126.0
FIXED_REF_MAX_BOUND = 24.0
VMEM_LIMIT = 56 * 2 ** 20

FOX_BLOCK = 512
SB_BLOCK = 256
ROW_TILE = 512
FFN_ROW_TILE = 1024
FFN_COL_TILE = 512


def _cparams(*sem):
    return pltpu.CompilerParams(dimension_semantics=sem, vmem_limit_bytes=VMEM_LIMIT)


def _dot(a, b, **kw):
    return jnp.dot(a, b, preferred_element_type=F32, **kw)


def _dot_nt(a, b):
    return lax.dot_general(a, b, (((1,), (1,)), ((), ())), preferred_element_type=F32)


def _normmod(x, gain, shift, scale):
    ms = jnp.mean(x * x, axis=-1, keepdims=True)
    return x * lax.rsqrt(ms + EPS) * gain * (1.0 + scale) + shift


def _split_bf16(x):
    hi = x.astype(BF)
    return hi, (x - hi.astype(F32)).astype(BF)


def _dot_3pass(a, w_hi, w_lo):
    a_hi, a_lo = _split_bf16(a)
    return _dot(a_hi, w_hi) + _dot(a_lo, w_hi) + _dot(a_hi, w_lo)


def _softplus(x):
    return jnp.maximum(x, 0.0) + jnp.log1p(jnp.exp(-jnp.abs(x)))


def _mod_spec(mod, tm):
    d = mod.shape[-1]
    if mod.shape[1] == 1:
        return pl.BlockSpec((1, 1, d), lambda b, i, *_: (b, 0, 0))
    return pl.BlockSpec((1, tm, d), lambda b, i, *_: (b, i, 0))


def _const_spec(shape):
    zeros = (0,) * len(shape)
    return pl.BlockSpec(shape, lambda *_: zeros)


def _ada_kernel(c_ref, w_ref, b_ref, o_ref):
    c = c_ref[...]
    c_act = (c * jax.nn.sigmoid(c)).astype(BF)
    o_ref[0] = _dot(c_act, w_ref[0].astype(BF)) + b_ref[0]


def _ada(c_all, w_ada, b_ada):
    rows, d = c_all.shape
    depth, _, n = w_ada.shape
    tn = 1024
    return pl.pallas_call(
        _ada_kernel,
        grid=(depth, n // tn),
        in_specs=[pl.BlockSpec((rows, d), lambda l, j: (0, 0)),
                  pl.BlockSpec((1, d, tn), lambda l, j: (l, 0, j)),
                  pl.BlockSpec((1, 1, tn), lambda l, j: (l, 0, j))],
        out_specs=pl.BlockSpec((1, rows, tn), lambda l, j: (l, 0, j)),
        out_shape=jax.ShapeDtypeStruct((depth, rows, n), F32),
        compiler_params=_cparams("parallel", "parallel"),
        name="ada_modulation",
    )(c_all, w_ada, b_ada.reshape(depth, 1, n))


def _qkv_kernel(*refs, fox, aliased):
    refs = list(refs)
    x_ref, g_ref, sh_ref, sc_ref, w_ref = refs[:5]
    del refs[:5]
    if fox:
        wf_hi_ref, wf_lo_ref, bf_ref, qg_ref, kg_ref = refs[:5]
        del refs[:5]
    if aliased:
        del refs[:2]
    q_ref, k32_ref, kbf_ref, v32_ref, vbf_ref = refs[:5]
    h = _normmod(x_ref[0], g_ref[...], sh_ref[0], sc_ref[0])
    hb = h.astype(BF)
    tm = hb.shape[0]
    k_rows = k32_ref.reshape(1, tm * N_HEADS, HEAD_DIM)
    v_rows = v32_ref.reshape(1, tm * N_HEADS, HEAD_DIM)

    def head_norm(a, gain_ref):
        ms = jnp.mean(a * a, axis=-1, keepdims=True)
        return a * lax.rsqrt(ms + EPS) * gain_ref[...]

    for c in range(3 * D_MODEL // MXU_N):
        acc = _dot(hb, w_ref[:, c * MXU_N:(c + 1) * MXU_N])
        part = (c * MXU_N) // D_MODEL
        for hh in range(MXU_N // HEAD_DIM):
            col = (c * MXU_N) % D_MODEL + hh * HEAD_DIM
            sl = slice(col, col + HEAD_DIM)
            state_rows = pl.ds(col // HEAD_DIM, tm, stride=N_HEADS)
            a = acc[:, hh * HEAD_DIM:(hh + 1) * HEAD_DIM]
            if part == 0:
                if fox:
                    a = head_norm(a, qg_ref)
                q_ref[0, :, sl] = (a * Q_SCALE).astype(BF)
            elif part == 1:
                if fox:
                    a = head_norm(a, kg_ref)
                k_rows[0, state_rows, :] = a
                kbf_ref[0, :, sl] = a.astype(BF)
            else:
                v_rows[0, state_rows, :] = a
                vbf_ref[0, :, sl] = a.astype(BF)
    if fox:
        f_logit = _dot_3pass(h, wf_hi_ref[...], wf_lo_ref[...]) + bf_ref[...]
        refs[5][0] = -_softplus(-f_logit)


def _qkv_proj(x, gain, shift, scale, w_qkv, fox_params=None, n_slots=1, slot=0, state=None):
    B, T, D = x.shape
    tm = min(T, ROW_TILE)
    fox = fox_params is not None
    aliased = state is not None
    row = pl.BlockSpec((1, tm, D), lambda b, i: (b, i, 0))
    state_spec = pl.BlockSpec((None, 1, tm, N_HEADS, HEAD_DIM), lambda b, i: (slot, b, i, 0, 0))
    state_shape = jax.ShapeDtypeStruct((n_slots, B, T, N_HEADS, HEAD_DIM), F32)
    in_specs = [row, _const_spec((1, D)), _mod_spec(shift, tm), _mod_spec(scale, tm), _const_spec((D, 3 * D))]
    args = [x, gain.reshape(1, D), shift, scale, w_qkv]
    out_specs = [row, state_spec, row, state_spec, row]
    out_shape = [jax.ShapeDtypeStruct((B, T, D), BF), state_shape, jax.ShapeDtypeStruct((B, T, D), BF),
                 state_shape, jax.ShapeDtypeStruct((B, T, D), BF)]
    if fox:
        w_f, b_f, q_gain, k_gain = fox_params
        in_specs += [_const_spec((D, LANES)), _const_spec((D, LANES)), _const_spec((1, LANES)),
                     _const_spec((1, HEAD_DIM)), _const_spec((1, HEAD_DIM))]
        args += [*_split_bf16(jnp.pad(w_f, ((0, 0), (0, LANES - N_HEADS)))),
                 jnp.pad(b_f, (0, LANES - N_HEADS)).reshape(1, LANES),
                 q_gain.reshape(1, HEAD_DIM), k_gain.reshape(1, HEAD_DIM)]
        out_specs.append(pl.BlockSpec((1, tm, LANES), lambda b, i: (b, i, 0)))
        out_shape.append(jax.ShapeDtypeStruct((B, T, LANES), F32))
    aliases = {}
    if aliased:
        aliases = {len(args): 1, len(args) + 1: 3}
        in_specs += [pl.BlockSpec(memory_space=pl.ANY)] * 2
        args += list(state)
    return pl.pallas_call(
        functools.partial(_qkv_kernel, fox=fox, aliased=aliased),
        grid=(B, T // tm),
        in_specs=in_specs, out_specs=out_specs, out_shape=out_shape,
        input_output_aliases=aliases,
        compiler_params=_cparams("parallel", "parallel"),
        name="fox_qkv_proj" if fox else "sb_qkv_proj",
    )(*args)


def _split3_bf16(x):
    hi = x.astype(BF).astype(F32)
    mid = (x - hi).astype(BF).astype(F32)
    return hi, mid, (x - hi - mid).astype(BF).astype(F32)


def _fox_prep_kernel(bound_ref, lf_ref, qx_ref, kx_ref, cum_ref, bmin_ref, bmax_ref, carry_sc):
    i = pl.program_id(1)
    tm = lf_ref.shape[1]

    @pl.when(i == 0)
    def _():
        carry_sc[...] = jnp.zeros_like(carry_sc)

    r = lax.broadcasted_iota(jnp.int32, (tm, tm), 0)
    c = lax.broadcasted_iota(jnp.int32, (tm, tm), 1)
    before = (c <= r).astype(BF)
    cum = carry_sc[...]
    for piece in _split3_bf16(lf_ref[0]):
        cum = cum + _dot(before, piece.astype(BF))
    carry_sc[...] = cum[tm - 1:tm, :]
    cum_ref[0] = cum
    bmin_ref[0, 0] = jnp.min(cum, axis=0, keepdims=True)
    bmax_ref[0, 0] = jnp.max(cum, axis=0, keepdims=True)
    c2 = cum * LOG2E
    bound2 = bound_ref[0] * LOG2E
    lane = lax.broadcasted_iota(jnp.int32, (tm, LANES), 1)
    for h in range(N_HEADS):
        col = jnp.broadcast_to(c2[:, h:h + 1], (tm, LANES))
        k1, k2, k3 = _split3_bf16(-col)
        q1, q2, q3 = _split3_bf16(col - bound2)
        kx = jnp.where(lane == 0, k1, jnp.where(lane == 1, k2, jnp.where(lane == 2, k3,
                                                                         jnp.where(lane < 6, 1.0, 0.0))))
        qx = jnp.where(lane < 3, 1.0, jnp.where(lane == 3, q1, jnp.where(lane == 4, q2,
                                                                         jnp.where(lane == 5, q3, 0.0))))
        sl = slice(h * LANES, (h + 1) * LANES)
        qx_ref[0, :, sl] = qx.astype(BF)
        kx_ref[0, :, sl] = kx.astype(BF)


def _fox_prep(logf, bound, blk):
    B, T, _ = logf.shape
    nb = T // blk
    row = pl.BlockSpec((1, blk, LANES), lambda b, i: (b, i, 0))
    wide = pl.BlockSpec((1, blk, N_HEADS * LANES), lambda b, i: (b, i, 0))
    stat = pl.BlockSpec((1, 1, 1, LANES), lambda b, i: (b, i, 0, 0))
    return pl.pallas_call(
        _fox_prep_kernel, grid=(B, nb),
        in_specs=[pl.BlockSpec(memory_space=pltpu.SMEM), row],
        out_specs=[wide, wide, row, stat, stat],
        out_shape=[jax.ShapeDtypeStruct((B, T, N_HEADS * LANES), BF)] * 2
        + [jax.ShapeDtypeStruct((B, T, LANES), F32)] + [jax.ShapeDtypeStruct((B, nb, 1, LANES), F32)] * 2,
        scratch_shapes=[pltpu.VMEM((1, LANES), F32)],
        compiler_params=_cparams("parallel", "arbitrary"),
        name="fox_prep",
    )(bound.reshape(1), logf)


def _cumsum_flat_kernel(x_ref, o_ref):
    n = x_ref.shape[-1]
    r = lax.broadcasted_iota(jnp.int32, (n, n), 0)
    c = lax.broadcasted_iota(jnp.int32, (n, n), 1)
    o_ref[...] = _dot(x_ref[...], (r <= c).astype(F32), precision=HIGHEST)


def _cumsum_flat(x):
    return pl.pallas_call(
        _cumsum_flat_kernel, out_shape=jax.ShapeDtypeStruct(x.shape, F32),
        compiler_params=pltpu.CompilerParams(vmem_limit_bytes=VMEM_LIMIT),
        name="cumsum_flat",
    )(x)


def _fox_attn_kernel(js_ref, q_ref, qx_ref, k_ref, kx_ref, v_ref, o_ref, acc_sc, *, blk):
    b, h, i = pl.program_id(0), pl.program_id(1), pl.program_id(2)
    nq = pl.num_programs(2)
    q = jnp.concatenate([q_ref[0], qx_ref[0]], axis=-1)
    acc_sc[...] = jnp.zeros_like(acc_sc)
    lane = lax.broadcasted_iota(jnp.int32, (blk, HEAD_DIM), 1)
    ones_col = (lane == 0).astype(BF)

    def block(j, diagonal):
        start = pl.multiple_of(j * blk, blk)
        kb = jnp.concatenate([k_ref[0, pl.ds(start, blk), :], kx_ref[0, pl.ds(start, blk), :]], axis=-1)
        vb = jnp.concatenate([v_ref[0, pl.ds(start, blk), :], ones_col], axis=-1)
        s = _dot_nt(q, kb)
        if diagonal:
            row = lax.broadcasted_iota(jnp.int32, (blk, blk), 0)
            col = lax.broadcasted_iota(jnp.int32, (blk, blk), 1)
            s = jnp.where(col <= row, s, -jnp.inf)
        acc_sc[...] += _dot(jnp.exp2(s).astype(BF), vb)

    j0 = js_ref[(b * N_HEADS + h) * nq + i]
    n_before = i - j0

    def pair(t, carry):
        block(j0 + 2 * t, False)
        block(j0 + 2 * t + 1, False)
        return carry

    lax.fori_loop(0, n_before // 2, pair, 0)

    @pl.when(n_before % 2 == 1)
    def _():
        block(i - 1, False)
        block(i, True)

    @pl.when(n_before % 2 == 0)
    def _():
        block(i, True)

    acc = acc_sc[...]
    o_ref[0] = (acc[:, :HEAD_DIM] / acc[:, HEAD_DIM:HEAD_DIM + 1]).astype(BF)


def _fox_attn_online_kernel(js_ref, q_ref, k_ref, v_ref, ck_ref, o_ref, m_sc, l_sc, acc_sc, *, blk):
    b, h, i = pl.program_id(0), pl.program_id(1), pl.program_id(2)
    nq = pl.num_programs(2)
    q = q_ref[0]
    m_sc[...] = jnp.full_like(m_sc, -jnp.inf)
    l_sc[...] = jnp.zeros_like(l_sc)
    acc_sc[...] = jnp.zeros_like(acc_sc)

    def block(j, diagonal):
        start = pl.multiple_of(j * blk, blk)
        kb = k_ref[0, pl.ds(start, blk), :]
        vb = v_ref[0, pl.ds(start, blk), :]
        s = _dot_nt(q, kb) - ck_ref[0, j]
        if diagonal:
            row = lax.broadcasted_iota(jnp.int32, (blk, blk), 0)
            col = lax.broadcasted_iota(jnp.int32, (blk, blk), 1)
            s = jnp.where(col <= row, s, -jnp.inf)
        m_prev = m_sc[...]
        m_new = jnp.maximum(m_prev, jnp.max(s, axis=-1, keepdims=True))
        alpha = jnp.exp2(m_prev - m_new)
        p = jnp.exp2(s - m_new)
        l_sc[...] = alpha * l_sc[...] + jnp.sum(p, axis=-1, keepdims=True)
        acc_sc[...] = alpha * acc_sc[...] + _dot(p.astype(BF), vb)
        m_sc[...] = m_new

    def body(j, carry):
        block(j, False)
        return carry

    lax.fori_loop(js_ref[(b * N_HEADS + h) * nq + i], i, body, 0)
    block(i, True)
    o_ref[0] = (acc_sc[...] / l_sc[...]).astype(BF)


def _fox_first_blocks(block_min, block_max, dead_gap):
    nb = block_min.shape[-1]
    j = jnp.arange(nb)
    upto = j[None, :] <= j[:, None]
    prefix_min = jnp.min(jnp.where(upto, block_min[..., None, :], jnp.inf), axis=-1)
    dead = (prefix_min[..., None, :] - block_max[..., :, None]) > dead_gap
    return jnp.sum(dead & (j[None, :] < j[:, None]), axis=-1).astype(jnp.int32)


def _fox_attention(q, kbf, vbf, logf, logit_bound, blk):
    B, T, D = q.shape
    nq = T // blk
    qx, kx, cum, bmin, bmax = _fox_prep(logf, logit_bound, blk)
    heads_first = lambda a: jnp.swapaxes(a[:, :, 0, :N_HEADS], 1, 2)
    use_fixed_reference = logit_bound < FIXED_REF_MAX_BOUND
    dead_gap = jnp.where(use_fixed_reference, (EXP2_FLUSH + 1.0) / LOG2E,
                         2.0 * logit_bound + EXP_UNDERFLOW + 1.0)
    first_blocks = _fox_first_blocks(heads_first(bmin), heads_first(bmax), dead_gap).reshape(-1)
    qspec = pl.BlockSpec((1, blk, HEAD_DIM), lambda b, h, i, js: (b, i, h))
    kvspec = pl.BlockSpec((1, T, HEAD_DIM), lambda b, h, i, js: (b, 0, h))
    out_shape = jax.ShapeDtypeStruct((B, T, D), BF)

    def fixed_reference():
        return pl.pallas_call(
            functools.partial(_fox_attn_kernel, blk=blk),
            grid_spec=pltpu.PrefetchScalarGridSpec(
                num_scalar_prefetch=1, grid=(B, N_HEADS, nq),
                in_specs=[qspec, qspec, kvspec, kvspec, kvspec], out_specs=qspec,
                scratch_shapes=[pltpu.VMEM((blk, 2 * HEAD_DIM), F32)]),
            out_shape=out_shape,
            compiler_params=_cparams("parallel", "parallel", "arbitrary"),
            name="fox_attention",
        )(first_blocks, q, qx, kbf, kx, vbf)

    def online():
        ck = jnp.swapaxes(cum[:, :, :N_HEADS], 1, 2) * LOG2E
        ckspec = pl.BlockSpec((1, nq, 1, blk), lambda b, h, i, js: (b * N_HEADS + h, 0, 0, 0))
        return pl.pallas_call(
            functools.partial(_fox_attn_online_kernel, blk=blk),
            grid_spec=pltpu.PrefetchScalarGridSpec(
                num_scalar_prefetch=1, grid=(B, N_HEADS, nq),
                in_specs=[qspec, kvspec, kvspec, ckspec], out_specs=qspec,
                scratch_shapes=[pltpu.VMEM((blk, 1), F32), pltpu.VMEM((blk, 1), F32),
                                pltpu.VMEM((blk, HEAD_DIM), F32)]),
            out_shape=out_shape,
            compiler_params=_cparams("parallel", "parallel", "arbitrary"),
            name="fox_attention_online",
        )(first_blocks, q, kbf, vbf, ck.reshape(B * N_HEADS, nq, 1, blk))

    return lax.cond(use_fixed_reference, fixed_reference, online)


def _fox_attn_past_kernel(q_ref, kn_ref, vn_ref, kp_ref, vp_ref, cp_ref, cn_ref, o_ref):
    tn = q_ref.shape[1]
    row = lax.broadcasted_iota(jnp.int32, (tn, tn), 0)
    col = lax.broadcasted_iota(jnp.int32, (tn, tn), 1)
    for h in range(N_HEADS):
        sl = slice(h * HEAD_DIM, (h + 1) * HEAD_DIM)
        q = q_ref[0, :, sl]
        sp = _dot_nt(q, kp_ref[0, :, sl].astype(BF)) - cp_ref[0, h:h + 1, :]
        sn = _dot_nt(q, kn_ref[0, :, sl]) - cn_ref[0, h:h + 1, :]
        sn = jnp.where(col <= row, sn, -jnp.inf)
        m = jnp.maximum(jnp.max(sp, axis=-1, keepdims=True), jnp.max(sn, axis=-1, keepdims=True))
        pp = jnp.exp2(sp - m)
        pn = jnp.exp2(sn - m)
        l = jnp.sum(pp, axis=-1, keepdims=True) + jnp.sum(pn, axis=-1, keepdims=True)
        o = _dot(pp.astype(BF), vp_ref[0, :, sl].astype(BF)) + _dot(pn.astype(BF), vn_ref[0, :, sl])
        o_ref[0, :, sl] = (o / l).astype(BF)


def _fox_attention_past(q, kn, vn, cache_k, cache_v, layer, cum_past, cum_new):
    B, Tn, D = q.shape
    Tp = cache_k.shape[2]
    new = pl.BlockSpec((1, Tn, D), lambda b: (b, 0, 0))
    past = pl.BlockSpec((None, 1, Tp, D), lambda b: (layer, b, 0, 0))
    return pl.pallas_call(
        _fox_attn_past_kernel, grid=(B,),
        in_specs=[new, new, new, past, past,
                  pl.BlockSpec((1, N_HEADS, Tp), lambda b: (b, 0, 0)),
                  pl.BlockSpec((1, N_HEADS, Tn), lambda b: (b, 0, 0))],
        out_specs=new, out_shape=jax.ShapeDtypeStruct((B, Tn, D), BF),
        compiler_params=_cparams("parallel"),
        name="fox_attention_past",
    )(q, kn, vn, cache_k, cache_v, cum_past, cum_new)


def _neg_log2_rest(z2):
    zc = jnp.minimum(z2, 30.0)
    return jnp.log(1.0 + jnp.exp2(zc)) * LOG2E + (z2 - zc)


class _SbBlock:
    def __init__(self, blk):
        self.row = lax.broadcasted_iota(jnp.int32, (blk, blk), 0)
        self.col = lax.broadcasted_iota(jnp.int32, (blk, blk), 1)
        suffix = jnp.where(self.row >= self.col, -1.0, 0.0).astype(BF)
        self.suffix2 = jnp.concatenate([suffix, suffix], axis=0)

    def scores(self, q, kb, diagonal):
        z2 = _dot_nt(q, kb)
        nlr = _neg_log2_rest(z2)
        if diagonal:
            nlr = jnp.where(self.col < self.row, nlr, 0.0)
        hi, lo = _split_bf16(nlr)
        tail = _dot(jnp.concatenate([hi, lo], axis=-1), self.suffix2)
        return z2 + tail, tail[:, 0:1]

    def weighted(self, e, later, vb, diagonal):
        a = jnp.exp2(e + later)
        if diagonal:
            a = jnp.where(self.col < self.row, a, 0.0)
        return _dot(a.astype(BF), vb)


SB_HEADS_PER_STEP = 2
SB_FIRST_BLOCKS = 2


def _sb_attn_kernel(q_ref, k_ref, v_ref, o_ref, later_sc, acc_sc, *, blk):
    i = pl.program_id(2)
    tile = _SbBlock(blk)

    def head_cols(hh):
        return slice(hh * HEAD_DIM, (hh + 1) * HEAD_DIM)

    def kv(ref, j, hh):
        return ref[0, pl.ds(pl.multiple_of(j * blk, blk), blk), head_cols(hh)]

    def first_chunk(n_blocks):
        for hh in range(SB_HEADS_PER_STEP):
            q = q_ref[0, :, head_cols(hh)]
            later, acc = 0.0, None
            for back in range(n_blocks):
                e, total = tile.scores(q, kv(k_ref, i - back, hh), back == 0)
                part = tile.weighted(e, later, kv(v_ref, i - back, hh), back == 0)
                acc = part if acc is None else acc + part
                later = later + total
            acc_sc[hh] = acc
            later_sc[hh] = later

    for n_blocks in range(1, SB_FIRST_BLOCKS + 1):
        last = n_blocks == SB_FIRST_BLOCKS

        @pl.when((i >= n_blocks - 1) if last else (i == n_blocks - 1))
        def _(n_blocks=n_blocks):
            first_chunk(n_blocks)

    for hh in range(SB_HEADS_PER_STEP):
        q = q_ref[0, :, head_cols(hh)]

        def cond(carry):
            j, live = carry
            return jnp.logical_and(j >= 0, live > -EXP2_UNDERFLOW)

        def body(carry, hh=hh, q=q):
            j, _ = carry
            e, total = tile.scores(q, kv(k_ref, j, hh), False)
            later = later_sc[hh]
            acc_sc[hh] += tile.weighted(e, later, kv(v_ref, j, hh), False)
            later_sc[hh] = later + total
            return j - 1, jnp.max(later_sc[hh])

        lax.while_loop(cond, body, (i - SB_FIRST_BLOCKS, jnp.max(later_sc[hh])))
        o_ref[0, :, head_cols(hh)] = acc_sc[hh].astype(BF)


def _sb_attention(q, kbf, vbf, blk):
    B, T, D = q.shape
    width = SB_HEADS_PER_STEP * HEAD_DIM
    qspec = pl.BlockSpec((1, blk, width), lambda b, h, i: (b, i, h))
    kvspec = pl.BlockSpec((1, T, width), lambda b, h, i: (b, 0, h))
    return pl.pallas_call(
        functools.partial(_sb_attn_kernel, blk=blk),
        grid=(B, N_HEADS // SB_HEADS_PER_STEP, T // blk),
        in_specs=[qspec, kvspec, kvspec], out_specs=qspec,
        scratch_shapes=[pltpu.VMEM((SB_HEADS_PER_STEP, blk, 1), F32),
                        pltpu.VMEM((SB_HEADS_PER_STEP, blk, HEAD_DIM), F32)],
        out_shape=jax.ShapeDtypeStruct((B, T, D), BF),
        compiler_params=_cparams("parallel", "parallel", "arbitrary"),
        name="sb_attention",
    )(q, kbf, vbf)


def _sb_attn_past_kernel(q_ref, kn_ref, vn_ref, kp_ref, vp_ref, o_ref, zp_sc, lrp_sc, zn_sc, lrn_sc):
    tn = q_ref.shape[1]
    tp = kp_ref.shape[1]
    row = lax.broadcasted_iota(jnp.int32, (tn, tn), 0)
    col = lax.broadcasted_iota(jnp.int32, (tn, tn), 1)
    for h in range(N_HEADS):
        sl = slice(h * HEAD_DIM, (h + 1) * HEAD_DIM)
        rows = slice(h * tn, (h + 1) * tn)
        q = q_ref[0, :, sl]
        zp = _dot_nt(q, kp_ref[0, :, sl].astype(BF))
        zn = _dot_nt(q, kn_ref[0, :, sl])
        zp_sc[rows, :] = zp
        lrp_sc[rows, :] = -_neg_log2_rest(zp)
        zn_sc[rows, :] = zn
        lrn_sc[rows, :] = jnp.where(col < row, -_neg_log2_rest(zn), 0.0)
    rp = lax.broadcasted_iota(jnp.int32, (tp, tp), 0)
    cp = lax.broadcasted_iota(jnp.int32, (tp, tp), 1)
    lrp = lrp_sc[...]
    lrn = lrn_sc[...]
    hi, lo = _split_bf16(lrp)
    later_p = (rp > cp).astype(BF)
    between_p = _dot(hi, later_p) + _dot(lo, later_p) + jnp.sum(lrn, axis=-1, keepdims=True)
    hi, lo = _split_bf16(lrn)
    later_n = (row > col).astype(BF)
    between_n = _dot(hi, later_n) + _dot(lo, later_n)
    ap = jnp.exp2(zp_sc[...] + lrp + between_p).astype(BF)
    an = jnp.exp2(zn_sc[...] + lrn + between_n)
    for h in range(N_HEADS):
        sl = slice(h * HEAD_DIM, (h + 1) * HEAD_DIM)
        rows = slice(h * tn, (h + 1) * tn)
        a_new = jnp.where(col < row, an[rows, :], 0.0).astype(BF)
        o = _dot(ap[rows, :], vp_ref[0, :, sl].astype(BF)) + _dot(a_new, vn_ref[0, :, sl])
        o_ref[0, :, sl] = o.astype(BF)


def _sb_attention_past(q, kn, vn, cache_k, cache_v, layer):
    B, Tn, D = q.shape
    Tp = cache_k.shape[2]
    new = pl.BlockSpec((1, Tn, D), lambda b: (b, 0, 0))
    past = pl.BlockSpec((None, 1, Tp, D), lambda b: (layer, b, 0, 0))
    return pl.pallas_call(
        _sb_attn_past_kernel, grid=(B,),
        in_specs=[new, new, new, past, past], out_specs=new,
        scratch_shapes=[pltpu.VMEM((N_HEADS * Tn, Tp), F32), pltpu.VMEM((N_HEADS * Tn, Tp), F32),
                        pltpu.VMEM((N_HEADS * Tn, Tn), F32), pltpu.VMEM((N_HEADS * Tn, Tn), F32)],
        out_shape=jax.ShapeDtypeStruct((B, Tn, D), BF),
        compiler_params=_cparams("parallel"),
        name="sb_attention_past",
    )(q, kn, vn, cache_k, cache_v)


def _top2_gates(logits):
    lane = lax.broadcasted_iota(jnp.int32, logits.shape, 1)
    logits = jnp.where(lane < N_EXPERTS, logits, -jnp.inf)
    v1 = jnp.max(logits, axis=-1, keepdims=True)
    i1 = jnp.min(jnp.where(logits == v1, lane, LANES), axis=-1, keepdims=True)
    rest = jnp.where(lane == i1, -jnp.inf, logits)
    v2 = jnp.max(rest, axis=-1, keepdims=True)
    i2 = jnp.min(jnp.where(rest == v2, lane, LANES), axis=-1, keepdims=True)
    e2 = jnp.exp(v2 - v1)
    return jnp.where(lane == i1, 1.0 / (1.0 + e2), 0.0) + jnp.where(lane == i2, e2 / (1.0 + e2), 0.0)


def _ffn_kernel(*refs, moe, mixed):
    refs = list(refs)
    x_ref, g_ref, sh_ref, sc_ref, gate_ref = refs[:5]
    del refs[:5]
    if mixed:
        a_ref, wo_ref, mix_gate_ref = refs[:3]
        del refs[:3]
    if moe:
        wr_hi_ref, wr_lo_ref = refs[:2]
        del refs[:2]
    wg_ref, wu_ref, wd_ref, o_ref, h_sc, acc_sc = refs[:6]
    f = pl.program_id(2)

    @pl.when(f == 0)
    def _():
        x = x_ref[0]
        if mixed:
            x = x + mix_gate_ref[0] * _dot(a_ref[0], wo_ref[...])
        o_ref[0] = x
        h = _normmod(x, g_ref[...], sh_ref[0], sc_ref[0])
        h_sc[...] = h.astype(BF)
        acc_sc[...] = jnp.zeros_like(acc_sc)
        if moe:
            refs[6][...] = _top2_gates(_dot_3pass(h, wr_hi_ref[...], wr_lo_ref[...]))

    hb = h_sc[...]
    g = _dot(hb, wg_ref[...])
    u = _dot(hb, wu_ref[...])
    a = g * jax.nn.sigmoid(g) * u
    if moe:
        gates = refs[6][...]
        lane = lax.broadcasted_iota(jnp.int32, gates.shape, 1)
        a = a * jnp.sum(jnp.where(lane == f, gates, 0.0), axis=-1, keepdims=True)
    acc_sc[...] += _dot(a.astype(BF), wd_ref[...])

    @pl.when(f == pl.num_programs(2) - 1)
    def _():
        o_ref[0] = o_ref[0] + gate_ref[0] * acc_sc[...]


def _ffn(x, gain, shift, scale, gate, w_gate, w_up, w_down, w_router=None, mixer=None):
    B, T, D = x.shape
    tm = min(T, FFN_ROW_TILE)
    moe = w_router is not None
    mixed = mixer is not None
    row = pl.BlockSpec((1, tm, D), lambda b, i, f: (b, i, 0))
    in_specs = [row, _const_spec((1, D)), _mod_spec(shift, tm), _mod_spec(scale, tm), _mod_spec(gate, tm)]
    args = [x, gain.reshape(1, D), shift, scale, gate]
    if mixed:
        a, w_out, mix_gate = mixer
        in_specs += [pl.BlockSpec((1, tm, a.shape[-1]), lambda b, i, f: (b, i, 0)),
                     _const_spec(w_out.shape), _mod_spec(mix_gate, tm)]
        args += [a, w_out, mix_gate]
    scratch = [pltpu.VMEM((tm, D), BF), pltpu.VMEM((tm, D), F32)]
    if moe:
        nf, _, tf = w_gate.shape
        in_specs += [_const_spec((D, LANES)), _const_spec((D, LANES)),
                     pl.BlockSpec((None, D, tf), lambda b, i, f: (f, 0, 0)),
                     pl.BlockSpec((None, D, tf), lambda b, i, f: (f, 0, 0)),
                     pl.BlockSpec((None, tf, D), lambda b, i, f: (f, 0, 0))]
        args += list(_split_bf16(jnp.pad(w_router, ((0, 0), (0, LANES - N_EXPERTS)))))
        scratch.append(pltpu.VMEM((tm, LANES), F32))
    else:
        tf = FFN_COL_TILE
        nf = w_gate.shape[1] // tf
        in_specs += [pl.BlockSpec((D, tf), lambda b, i, f: (0, f)),
                     pl.BlockSpec((D, tf), lambda b, i, f: (0, f)),
                     pl.BlockSpec((tf, D), lambda b, i, f: (f, 0))]
    args += [w_gate, w_up, w_down]
    return pl.pallas_call(
        functools.partial(_ffn_kernel, moe=moe, mixed=mixed),
        grid=(B, T // tm, nf),
        in_specs=in_specs, out_specs=row, scratch_shapes=scratch,
        out_shape=jax.ShapeDtypeStruct((B, T, D), F32),
        compiler_params=_cparams("parallel", "parallel", "arbitrary"),
        name="moe_ffn" if moe else "dense_ffn",
    )(*args)


def _lru_proj_kernel(x_ref, g_ref, sh_ref, sc_ref, w_ref, gate_ref, xr_ref, tail_ref, *, tail_rows):
    h = _normmod(x_ref[0], g_ref[...], sh_ref[0], sc_ref[0])
    hb = h.astype(BF)
    tm = hb.shape[0]
    for c in range(2 * D_RNN // MXU_N):
        acc = _dot(hb, w_ref[:, c * MXU_N:(c + 1) * MXU_N])
        if c * MXU_N < D_RNN:
            gate_ref[0, :, c * MXU_N:(c + 1) * MXU_N] = jax.nn.gelu(acc).astype(BF)
        else:
            sl = slice(c * MXU_N - D_RNN, (c + 1) * MXU_N - D_RNN)
            xr_ref[0, :, sl] = acc.astype(BF)
            tail_ref[0, :, sl] = acc[tm - tail_rows:, :]


def _lru_proj(x, gain, shift, scale, w_in, all_rows_f32):
    B, T, D = x.shape
    tm = min(T, ROW_TILE)
    tail_rows = tm if all_rows_f32 else SUBLANES
    tail_spec = pl.BlockSpec((1, tail_rows, D_RNN), (lambda b, i: (b, i, 0)) if all_rows_f32
                                       else (lambda b, i: (b, 0, 0)))
    row = pl.BlockSpec((1, tm, D), lambda b, i: (b, i, 0))
    rnn = pl.BlockSpec((1, tm, D_RNN), lambda b, i: (b, i, 0))
    return pl.pallas_call(
        functools.partial(_lru_proj_kernel, tail_rows=tail_rows), grid=(B, T // tm),
        in_specs=[row, _const_spec((1, D)), _mod_spec(shift, tm), _mod_spec(scale, tm),
                  _const_spec((D, 2 * D_RNN))],
        out_specs=[rnn, rnn, tail_spec],
        out_shape=[jax.ShapeDtypeStruct((B, T, D_RNN), BF), jax.ShapeDtypeStruct((B, T, D_RNN), BF),
                   jax.ShapeDtypeStruct((B, T if all_rows_f32 else SUBLANES, D_RNN), F32)],
        compiler_params=_cparams("parallel", "arbitrary"),
        name="lru_proj",
    )(x, gain.reshape(1, D), shift, scale, w_in)


def _lru_core_kernel(gg_ref, xr_ref, x_ref, gate_ref, h0_ref, prev_ref, cw_ref, cb_ref, wri_ref, bri_ref,
                     lam_ref, wo_ref, o_ref, hl_ref, xp_sc, a_sc, b_sc, hs_sc, h_sc):
    i = pl.program_id(1)
    tm = xr_ref.shape[1]

    @pl.when(i == 0)
    def _():
        xp_sc[0:SUBLANES, :] = prev_ref[0]
        h_sc[...] = h0_ref[0]

    xp_sc[SUBLANES:, :] = xr_ref[0].astype(F32)
    xc = cb_ref[...]
    for j in range(CONV_W):
        off = SUBLANES - (CONV_W - 1) + j
        xc = xc + cw_ref[j:j + 1, :] * xp_sc[off:off + tm, :]
    xp_sc[0:SUBLANES, :] = xp_sc[tm:tm + SUBLANES, :]

    xcb = xc.astype(BF)
    neg_decay = -LRU_C * _softplus(-lam_ref[...])
    for s in range(D_RNN // RNN_SUPER):
        sl = slice(s * RNN_SUPER, (s + 1) * RNN_SUPER)
        z = _dot(xcb[:, sl], wri_ref[s])
        r = jax.nn.sigmoid(z[:, :RNN_SUPER] + bri_ref[0:1, sl])
        g = jax.nn.sigmoid(z[:, RNN_SUPER:] + bri_ref[1:2, sl])
        log_a = neg_decay[:, sl] * r
        y = 2.0 * log_a
        a_sc[:, sl] = jnp.exp(log_a)
        b_sc[:, sl] = jnp.sqrt(-jnp.tanh(log_a) * (jnp.exp(y) + 1.0)) * (g * xc[:, sl])

    def step(t, h):
        h = a_sc[pl.ds(t, 1), :] * h + b_sc[pl.ds(t, 1), :]
        hs_sc[pl.ds(t, 1), :] = h
        return h

    h_last = lax.fori_loop(0, tm, step, h_sc[...], unroll=8)
    h_sc[...] = h_last
    hl_ref[0] = h_last
    y = (gg_ref[0].astype(F32) * hs_sc[...]).astype(BF)
    o_ref[0] = x_ref[0] + gate_ref[0] * _dot(y, wo_ref[...])


def _block_diag(w):
    n_super = D_RNN // RNN_SUPER
    per = N_RNN_BLOCKS // n_super
    w = w.reshape(n_super, per, w.shape[1], w.shape[2])
    return jnp.einsum('sbij,bc->sbicj', w, jnp.eye(per, dtype=w.dtype)).reshape(n_super, RNN_SUPER, RNN_SUPER)


def _lru_core(gg, xr, x, gate, h0, prev, conv_w, conv_b, w_r, b_r, w_i, b_i, lam, w_out):
    B, T, D = x.shape
    tm = min(T, ROW_TILE)
    w_ri = jnp.concatenate([_block_diag(w_r), _block_diag(w_i)], axis=-1).astype(BF)
    b_ri = jnp.stack([b_r, b_i])
    row = pl.BlockSpec((1, tm, D), lambda b, i: (b, i, 0))
    rnn = pl.BlockSpec((1, tm, D_RNN), lambda b, i: (b, i, 0))
    n_super = D_RNN // RNN_SUPER
    return pl.pallas_call(
        _lru_core_kernel, grid=(B, T // tm),
        in_specs=[rnn, rnn, row, _mod_spec(gate, tm),
                  pl.BlockSpec((1, 1, D_RNN), lambda b, i: (b, 0, 0)),
                  pl.BlockSpec((1, SUBLANES, D_RNN), lambda b, i: (b, 0, 0)),
                  _const_spec((CONV_W, D_RNN)), _const_spec((1, D_RNN)),
                  _const_spec((n_super, RNN_SUPER, 2 * RNN_SUPER)), _const_spec((2, D_RNN)),
                  _const_spec((1, D_RNN)), _const_spec((D_RNN, D))],
        out_specs=[row, pl.BlockSpec((1, 1, D_RNN), lambda b, i: (b, 0, 0))],
        out_shape=[jax.ShapeDtypeStruct((B, T, D), F32), jax.ShapeDtypeStruct((B, 1, D_RNN), F32)],
        scratch_shapes=[pltpu.VMEM((tm + SUBLANES, D_RNN), F32), pltpu.VMEM((tm, D_RNN), F32),
                        pltpu.VMEM((tm, D_RNN), F32), pltpu.VMEM((tm, D_RNN), F32),
                        pltpu.VMEM((1, D_RNN), F32)],
        compiler_params=_cparams("parallel", "arbitrary"),
        name="lru_core",
    )(gg, xr, x, gate, h0, prev, conv_w, conv_b.reshape(1, D_RNN), w_ri, b_ri, lam.reshape(1, D_RNN), w_out)


def _trunk(x, mod, W, past):
    B, T, D = x.shape
    depth = mod.shape[0]
    has_past = past is not None
    flat = (lambda a: a.reshape(1, B * T, a.shape[-1])) if has_past else (lambda a: a)
    unflat = (lambda a: a.reshape(B, T, a.shape[-1])) if has_past else (lambda a: a)
    n_fox = len(range(0, depth, N_MIXERS))
    n_sb = len(range(2, depth, N_MIXERS))
    fox_kv = sb_kv = None
    fox_logf, lru_new = [], []
    for i in range(depth):
        per_batch = [m[:, None, :] for m in jnp.split(mod[i], 6, axis=-1)]
        if has_past:
            per_row = [jnp.broadcast_to(m, (B, T, D)).reshape(1, B * T, D) for m in per_batch]
        else:
            per_row = per_batch
        shift1, scale1, gate1, shift2, scale2, gate2 = per_row
        kind, j = i % N_MIXERS, i // N_MIXERS
        if kind == 0:
            fox_params = (W['fox_w_f'][j], W['fox_b_f'][j], W['fox_q_norm'][j], W['fox_k_norm'][j])
            q, k32, kbf, v32, vbf, logf = _qkv_proj(flat(x), W['norm_mix'][i], shift1, scale1,
                                                    W['fox_w_qkv'][j], fox_params, n_fox, j, fox_kv)
            fox_kv = (k32, v32)
            q, kbf, vbf, logf = (unflat(a) for a in (q, kbf, vbf, logf))
            fox_logf.append(logf[..., :N_HEADS])
            if has_past:
                cache_k, cache_v, cache_lf = past['fox']
                Tp = cache_k.shape[2]
                lf_all = jnp.concatenate([jnp.swapaxes(cache_lf[j], 1, 2),
                                          jnp.swapaxes(fox_logf[-1], 1, 2)], axis=-1)
                pad = (-lf_all.shape[-1]) % LANES
                cum = _cumsum_flat(jnp.pad(lf_all, ((0, 0), (0, 0), (0, pad))).reshape(B * N_HEADS, -1))
                cum = cum.reshape(B, N_HEADS, -1) * LOG2E
                o = _fox_attention_past(q, kbf, vbf, cache_k.reshape(*cache_k.shape[:3], D),
                                        cache_v.reshape(*cache_v.shape[:3], D), j,
                                        cum[..., :Tp], cum[..., Tp:Tp + T])
            else:
                bound = 1.02 * HEAD_DIM ** 0.5 * jnp.max(jnp.abs(W['fox_q_norm'][j])) \
                    * jnp.max(jnp.abs(W['fox_k_norm'][j]))
                o = _fox_attention(q, kbf, vbf, logf, bound, min(T, FOX_BLOCK))
            mixer = (flat(o), W['fox_w_out'][j], gate1)
        elif kind == 1:
            gg, xr, tail = _lru_proj(flat(x), W['norm_mix'][i], shift1, scale1, W['lru_w_in'][j], has_past)
            gg, xr, tail = unflat(gg), unflat(xr), unflat(tail) if has_past else tail
            if has_past:
                state_h, state_conv = past['lru']
                h0 = state_h[j][:, None, :]
                prev = jnp.pad(state_conv[j], ((0, 0), (SUBLANES - (CONV_W - 1), 0), (0, 0)))
            else:
                h0 = jnp.zeros((B, 1, D_RNN), F32)
                prev = jnp.zeros((B, SUBLANES, D_RNN), F32)
            x, h_last = _lru_core(gg, xr, x, per_batch[2], h0, prev, W['lru_conv_w'][j], W['lru_conv_b'][j],
                                  W['lru_w_r'][j], W['lru_b_r'][j], W['lru_w_i'][j], W['lru_b_i'][j],
                                  W['lru_lambda'][j], W['lru_w_out'][j])
            lru_new.append((h_last[:, 0], tail[:, tail.shape[1] - (CONV_W - 1):]))
            mixer = None
        else:
            q, k32, kbf, v32, vbf = _qkv_proj(flat(x), W['norm_mix'][i], shift1, scale1, W['sb_w_in'][j],
                                              None, n_sb, j, sb_kv)
            sb_kv = (k32, v32)
            q, kbf, vbf = (unflat(a) for a in (q, kbf, vbf))
            if has_past:
                cache_k, cache_v = past['sb']
                o = _sb_attention_past(q, kbf, vbf, cache_k.reshape(*cache_k.shape[:3], D),
                                       cache_v.reshape(*cache_v.shape[:3], D), j)
            else:
                o = _sb_attention(q, kbf, vbf, min(T, SB_BLOCK))
            mixer = (flat(o), W['sb_w_out'][j], gate1)
        if i % 2 == 0:
            x = unflat(_ffn(flat(x), W['norm_ffn'][i], shift2, scale2, gate2,
                            W['ffn_w_gate'][i // 2], W['ffn_w_up'][i // 2], W['ffn_w_down'][i // 2],
                            None, mixer))
        else:
            x = unflat(_ffn(flat(x), W['norm_ffn'][i], shift2, scale2, gate2,
                            W['moe_w_gate'][i // 2], W['moe_w_up'][i // 2], W['moe_w_down'][i // 2],
                            W['moe_w_router'][i // 2], mixer))

    def heads(state):
        return state.reshape(state.shape[0], B, T, N_HEADS, HEAD_DIM)

    fox_state = [heads(fox_kv[0]), heads(fox_kv[1]), jnp.stack(fox_logf)]
    lru_state = [jnp.stack([s[n] for s in lru_new]) for n in range(2)]
    sb_state = [heads(sb_kv[0]), heads(sb_kv[1])]
    return x, fox_state, lru_state, sb_state


MATMUL_WEIGHTS = ('fox_w_qkv', 'fox_w_out', 'lru_w_in', 'lru_w_out', 'sb_w_in', 'sb_w_out',
                  'ffn_w_gate', 'ffn_w_up', 'ffn_w_down', 'moe_w_gate', 'moe_w_up', 'moe_w_down')


def _prepare_weights(W):
    W = dict(W)
    w_in = W.pop('fox_w_in')
    W['fox_w_qkv'] = w_in[:, :, :3 * D_MODEL]
    W['fox_w_f'] = w_in[:, :, 3 * D_MODEL:]
    for name in MATMUL_WEIGHTS:
        W[name] = W[name].astype(BF)
    return W


def _forward(W, x_prompt, c_prompt, x_sample, c_sample, past):
    W = _prepare_weights(W)
    n_prompt, n_sample = c_prompt.shape[0], c_sample.shape[0]
    c_all = jnp.concatenate([c_prompt, c_sample], axis=0)
    c_all = jnp.pad(c_all, ((0, (-c_all.shape[0]) % (2 * SUBLANES)), (0, 0)))
    mod = _ada(c_all, W['w_ada'], W['b_ada'])
    y_p, fox_p, lru_p, sb_p = _trunk(x_prompt, mod[:, :n_prompt], W, None)
    y_s, fox_s, lru_s, sb_s = _trunk(x_sample, mod[:, n_prompt:n_prompt + n_sample], W, past)

    return (y_p, y_s, *fox_p, *fox_s, *lru_p, *lru_s, *sb_p, *sb_s)


def kernel(x_prompt, x_sample, cache_fox_k, cache_fox_v, cache_fox_logf, state_lru_h, state_lru_conv,
           cache_sb_k, cache_sb_v, c_prompt, c_sample, norm_mix, norm_ffn, w_ada, b_ada,
           fox_w_in, fox_b_f, fox_q_norm, fox_k_norm, fox_w_out,
           lru_w_in, lru_conv_w, lru_conv_b, lru_w_r, lru_b_r, lru_w_i, lru_b_i, lru_lambda, lru_w_out,
           sb_w_in, sb_w_out, ffn_w_gate, ffn_w_up, ffn_w_down,
           moe_w_router, moe_w_gate, moe_w_up, moe_w_down):
    W = {
        'norm_mix': norm_mix, 'norm_ffn': norm_ffn, 'w_ada': w_ada, 'b_ada': b_ada,
        'fox_w_in': fox_w_in, 'fox_b_f': fox_b_f, 'fox_q_norm': fox_q_norm,
        'fox_k_norm': fox_k_norm, 'fox_w_out': fox_w_out,
        'lru_w_in': lru_w_in, 'lru_conv_w': lru_conv_w, 'lru_conv_b': lru_conv_b,
        'lru_w_r': lru_w_r, 'lru_b_r': lru_b_r, 'lru_w_i': lru_w_i, 'lru_b_i': lru_b_i,
        'lru_lambda': lru_lambda, 'lru_w_out': lru_w_out,
        'sb_w_in': sb_w_in, 'sb_w_out': sb_w_out,
        'ffn_w_gate': ffn_w_gate, 'ffn_w_up': ffn_w_up, 'ffn_w_down': ffn_w_down,
        'moe_w_router': moe_w_router, 'moe_w_gate': moe_w_gate, 'moe_w_up': moe_w_up,
        'moe_w_down': moe_w_down,
    }
    past = {'fox': (cache_fox_k, cache_fox_v, cache_fox_logf), 'lru': (state_lru_h, state_lru_conv),
            'sb': (cache_sb_k, cache_sb_v)}
    return _forward(W, x_prompt, c_prompt, x_sample, c_sample, past)
```

```python
import functools

import jax
import jax.numpy as jnp
from jax import lax
from jax.experimental import pallas as pl
from jax.experimental.pallas import tpu as pltpu

F32 = jnp.float32
BF = jnp.bfloat16
HIGHEST = lax.Precision.HIGHEST

D_MODEL = 1024
N_HEADS = 8
HEAD_DIM = 128
N_MIXERS = 3
D_RNN = 1280
N_RNN_BLOCKS = 16
RNN_SUPER = 640
CONV_W = 4
LRU_C = 8.0
N_EXPERTS = 8
EPS = 1e-6
ATTN_SCALE = HEAD_DIM ** -0.5
LANES = 128
SUBLANES = 8
MXU_N = 256
EXP_UNDERFLOW = 104.0
EXP2_UNDERFLOW = 150.0
LOG2E = 1.4426950408889634
Q_SCALE = ATTN_SCALE * LOG2E
EXP2_FLUSH = 126.0
FIXED_REF_MAX_BOUND = 24.0
VMEM_LIMIT = 56 * 2 ** 20

FOX_BLOCK = 512
SB_BLOCK = 256
ROW_TILE = 512
FFN_ROW_TILE = 1024
FFN_COL_TILE = 512


def _cparams(*sem):
    return pltpu.CompilerParams(dimension_semantics=sem, vmem_limit_bytes=VMEM_LIMIT)


def _dot(a, b, **kw):
    return jnp.dot(a, b, preferred_element_type=F32, **kw)


def _dot_nt(a, b):
    return lax.dot_general(a, b, (((1,), (1,)), ((), ())), preferred_element_type=F32)


def _normmod(x, gain, shift, scale):
    ms = jnp.mean(x * x, axis=-1, keepdims=True)
    return x * lax.rsqrt(ms + EPS) * gain * (1.0 + scale) + shift


def _split_bf16(x):
    hi = x.astype(BF)
    return hi, (x - hi.astype(F32)).astype(BF)


def _dot_3pass(a, w_hi, w_lo):
    a_hi, a_lo = _split_bf16(a)
    return _dot(a_hi, w_hi) + _dot(a_lo, w_hi) + _dot(a_hi, w_lo)


def _softplus(x):
    return jnp.maximum(x, 0.0) + jnp.log1p(jnp.exp(-jnp.abs(x)))


def _mod_spec(mod, tm):
    d = mod.shape[-1]
    if mod.shape[1] == 1:
        return pl.BlockSpec((1, 1, d), lambda b, i, *_: (b, 0, 0))
    return pl.BlockSpec((1, tm, d), lambda b, i, *_: (b, i, 0))


def _const_spec(shape):
    zeros = (0,) * len(shape)
    return pl.BlockSpec(shape, lambda *_: zeros)


def _ada_kernel(c_ref, w_ref, b_ref, o_ref):
    c = c_ref[...]
    c_act = (c * jax.nn.sigmoid(c)).astype(BF)
    o_ref[0] = _dot(c_act, w_ref[0].astype(BF)) + b_ref[0]


def _ada(c_all, w_ada, b_ada):
    rows, d = c_all.shape
    depth, _, n = w_ada.shape
    tn = 1024
    return pl.pallas_call(
        _ada_kernel,
        grid=(depth, n // tn),
        in_specs=[pl.BlockSpec((rows, d), lambda l, j: (0, 0)),
                  pl.BlockSpec((1, d, tn), lambda l, j: (l, 0, j)),
                  pl.BlockSpec((1, 1, tn), lambda l, j: (l, 0, j))],
        out_specs=pl.BlockSpec((1, rows, tn), lambda l, j: (l, 0, j)),
        out_shape=jax.ShapeDtypeStruct((depth, rows, n), F32),
        compiler_params=_cparams("parallel", "parallel"),
        name="ada_modulation",
    )(c_all, w_ada, b_ada.reshape(depth, 1, n))


def _qkv_kernel(*refs, fox, aliased):
    refs = list(refs)
    x_ref, g_ref, sh_ref, sc_ref, w_ref = refs[:5]
    del refs[:5]
    if fox:
        wf_hi_ref, wf_lo_ref, bf_ref, qg_ref, kg_ref = refs[:5]
        del refs[:5]
    if aliased:
        del refs[:2]
    q_ref, k32_ref, kbf_ref, v32_ref, vbf_ref = refs[:5]
    h = _normmod(x_ref[0], g_ref[...], sh_ref[0], sc_ref[0])
    hb = h.astype(BF)
    tm = hb.shape[0]
    k_rows = k32_ref.reshape(1, tm * N_HEADS, HEAD_DIM)
    v_rows = v32_ref.reshape(1, tm * N_HEADS, HEAD_DIM)

    def head_norm(a, gain_ref):
        ms = jnp.mean(a * a, axis=-1, keepdims=True)
        return a * lax.rsqrt(ms + EPS) * gain_ref[...]

    for c in range(3 * D_MODEL // MXU_N):
        acc = _dot(hb, w_ref[:, c * MXU_N:(c + 1) * MXU_N])
        part = (c * MXU_N) // D_MODEL
        for hh in range(MXU_N // HEAD_DIM):
            col = (c * MXU_N) % D_MODEL + hh * HEAD_DIM
            sl = slice(col, col + HEAD_DIM)
            state_rows = pl.ds(col // HEAD_DIM, tm, stride=N_HEADS)
            a = acc[:, hh * HEAD_DIM:(hh + 1) * HEAD_DIM]
            if part == 0:
                if fox:
                    a = head_norm(a, qg_ref)
                q_ref[0, :, sl] = (a * Q_SCALE).astype(BF)
            elif part == 1:
                if fox:
                    a = head_norm(a, kg_ref)
                k_rows[0, state_rows, :] = a
                kbf_ref[0, :, sl] = a.astype(BF)
            else:
                v_rows[0, state_rows, :] = a
                vbf_ref[0, :, sl] = a.astype(BF)
    if fox:
        f_logit = _dot_3pass(h, wf_hi_ref[...], wf_lo_ref[...]) + bf_ref[...]
        refs[5][0] = -_softplus(-f_logit)


def _qkv_proj(x, gain, shift, scale, w_qkv, fox_params=None, n_slots=1, slot=0, state=None):
    B, T, D = x.shape
    tm = min(T, ROW_TILE)
    fox = fox_params is not None
    aliased = state is not None
    row = pl.BlockSpec((1, tm, D), lambda b, i: (b, i, 0))
    state_spec = pl.BlockSpec((None, 1, tm, N_HEADS, HEAD_DIM), lambda b, i: (slot, b, i, 0, 0))
    state_shape = jax.ShapeDtypeStruct((n_slots, B, T, N_HEADS, HEAD_DIM), F32)
    in_specs = [row, _const_spec((1, D)), _mod_spec(shift, tm), _mod_spec(scale, tm), _const_spec((D, 3 * D))]
    args = [x, gain.reshape(1, D), shift, scale, w_qkv]
    out_specs = [row, state_spec, row, state_spec, row]
    out_shape = [jax.ShapeDtypeStruct((B, T, D), BF), state_shape, jax.ShapeDtypeStruct((B, T, D), BF),
                 state_shape, jax.ShapeDtypeStruct((B, T, D), BF)]
    if fox:
        w_f, b_f, q_gain, k_gain = fox_params
        in_specs += [_const_spec((D, LANES)), _const_spec((D, LANES)), _const_spec((1, LANES)),
                     _const_spec((1, HEAD_DIM)), _const_spec((1, HEAD_DIM))]
        args += [*_split_bf16(jnp.pad(w_f, ((0, 0), (0, LANES - N_HEADS)))),
                 jnp.pad(b_f, (0, LANES - N_HEADS)).reshape(1, LANES),
                 q_gain.reshape(1, HEAD_DIM), k_gain.reshape(1, HEAD_DIM)]
        out_specs.append(pl.BlockSpec((1, tm, LANES), lambda b, i: (b, i, 0)))
        out_shape.append(jax.ShapeDtypeStruct((B, T, LANES), F32))
    aliases = {}
    if aliased:
        aliases = {len(args): 1, len(args) + 1: 3}
        in_specs += [pl.BlockSpec(memory_space=pl.ANY)] * 2
        args += list(state)
    return pl.pallas_call(
        functools.partial(_qkv_kernel, fox=fox, aliased=aliased),
        grid=(B, T // tm),
        in_specs=in_specs, out_specs=out_specs, out_shape=out_shape,
        input_output_aliases=aliases,
        compiler_params=_cparams("parallel", "parallel"),
        name="fox_qkv_proj" if fox else "sb_qkv_proj",
    )(*args)


def _split3_bf16(x):
    hi = x.astype(BF).astype(F32)
    mid = (x - hi).astype(BF).astype(F32)
    return hi, mid, (x - hi - mid).astype(BF).astype(F32)


def _fox_prep_kernel(bound_ref, lf_ref, qx_ref, kx_ref, cum_ref, bmin_ref, bmax_ref, carry_sc):
    i = pl.program_id(1)
    tm = lf_ref.shape[1]

    @pl.when(i == 0)
    def _():
        carry_sc[...] = jnp.zeros_like(carry_sc)

    r = lax.broadcasted_iota(jnp.int32, (tm, tm), 0)
    c = lax.broadcasted_iota(jnp.int32, (tm, tm), 1)
    before = (c <= r).astype(BF)
    cum = carry_sc[...]
    for piece in _split3_bf16(lf_ref[0]):
        cum = cum + _dot(before, piece.astype(BF))
    carry_sc[...] = cum[tm - 1:tm, :]
    cum_ref[0] = cum
    bmin_ref[0, 0] = jnp.min(cum, axis=0, keepdims=True)
    bmax_ref[0, 0] = jnp.max(cum, axis=0, keepdims=True)
    c2 = cum * LOG2E
    bound2 = bound_ref[0] * LOG2E
    lane = lax.broadcasted_iota(jnp.int32, (tm, LANES), 1)
    for h in range(N_HEADS):
        col = jnp.broadcast_to(c2[:, h:h + 1], (tm, LANES))
        k1, k2, k3 = _split3_bf16(-col)
        q1, q2, q3 = _split3_bf16(col - bound2)
        kx = jnp.where(lane == 0, k1, jnp.where(lane == 1, k2, jnp.where(lane == 2, k3,
                                                                         jnp.where(lane < 6, 1.0, 0.0))))
        qx = jnp.where(lane < 3, 1.0, jnp.where(lane == 3, q1, jnp.where(lane == 4, q2,
                                                                         jnp.where(lane == 5, q3, 0.0))))
        sl = slice(h * LANES, (h + 1) * LANES)
        qx_ref[0, :, sl] = qx.astype(BF)
        kx_ref[0, :, sl] = kx.astype(BF)


def _fox_prep(logf, bound, blk):
    B, T, _ = logf.shape
    nb = T // blk
    row = pl.BlockSpec((1, blk, LANES), lambda b, i: (b, i, 0))
    wide = pl.BlockSpec((1, blk, N_HEADS * LANES), lambda b, i: (b, i, 0))
    stat = pl.BlockSpec((1, 1, 1, LANES), lambda b, i: (b, i, 0, 0))
    return pl.pallas_call(
        _fox_prep_kernel, grid=(B, nb),
        in_specs=[pl.BlockSpec(memory_space=pltpu.SMEM), row],
        out_specs=[wide, wide, row, stat, stat],
        out_shape=[jax.ShapeDtypeStruct((B, T, N_HEADS * LANES), BF)] * 2
        + [jax.ShapeDtypeStruct((B, T, LANES), F32)] + [jax.ShapeDtypeStruct((B, nb, 1, LANES), F32)] * 2,
        scratch_shapes=[pltpu.VMEM((1, LANES), F32)],
        compiler_params=_cparams("parallel", "arbitrary"),
        name="fox_prep",
    )(bound.reshape(1), logf)


def _cumsum_flat_kernel(x_ref, o_ref):
    n = x_ref.shape[-1]
    r = lax.broadcasted_iota(jnp.int32, (n, n), 0)
    c = lax.broadcasted_iota(jnp.int32, (n, n), 1)
    o_ref[...] = _dot(x_ref[...], (r <= c).astype(F32), precision=HIGHEST)


def _cumsum_flat(x):
    return pl.pallas_call(
        _cumsum_flat_kernel, out_shape=jax.ShapeDtypeStruct(x.shape, F32),
        compiler_params=pltpu.CompilerParams(vmem_limit_bytes=VMEM_LIMIT),
        name="cumsum_flat",
    )(x)


def _fox_attn_kernel(js_ref, q_ref, qx_ref, k_ref, kx_ref, v_ref, o_ref, acc_sc, *, blk):
    b, h, i = pl.program_id(0), pl.program_id(1), pl.program_id(2)
    nq = pl.num_programs(2)
    q = jnp.concatenate([q_ref[0], qx_ref[0]], axis=-1)
    acc_sc[...] = jnp.zeros_like(acc_sc)
    lane = lax.broadcasted_iota(jnp.int32, (blk, HEAD_DIM), 1)
    ones_col = (lane == 0).astype(BF)

    def block(j, diagonal):
        start = pl.multiple_of(j * blk, blk)
        kb = jnp.concatenate([k_ref[0, pl.ds(start, blk), :], kx_ref[0, pl.ds(start, blk), :]], axis=-1)
        vb = jnp.concatenate([v_ref[0, pl.ds(start, blk), :], ones_col], axis=-1)
        s = _dot_nt(q, kb)
        if diagonal:
            row = lax.broadcasted_iota(jnp.int32, (blk, blk), 0)
            col = lax.broadcasted_iota(jnp.int32, (blk, blk), 1)
            s = jnp.where(col <= row, s, -jnp.inf)
        acc_sc[...] += _dot(jnp.exp2(s).astype(BF), vb)

    j0 = js_ref[(b * N_HEADS + h) * nq + i]
    n_before = i - j0

    def pair(t, carry):
        block(j0 + 2 * t, False)
        block(j0 + 2 * t + 1, False)
        return carry

    lax.fori_loop(0, n_before // 2, pair, 0)

    @pl.when(n_before % 2 == 1)
    def _():
        block(i - 1, False)
        block(i, True)

    @pl.when(n_before % 2 == 0)
    def _():
        block(i, True)

    acc = acc_sc[...]
    o_ref[0] = (acc[:, :HEAD_DIM] / acc[:, HEAD_DIM:HEAD_DIM + 1]).astype(BF)


def _fox_attn_online_kernel(js_ref, q_ref, k_ref, v_ref, ck_ref, o_ref, m_sc, l_sc, acc_sc, *, blk):
    b, h, i = pl.program_id(0), pl.program_id(1), pl.program_id(2)
    nq = pl.num_programs(2)
    q = q_ref[0]
    m_sc[...] = jnp.full_like(m_sc, -jnp.inf)
    l_sc[...] = jnp.zeros_like(l_sc)
    acc_sc[...] = jnp.zeros_like(acc_sc)

    def block(j, diagonal):
        start = pl.multiple_of(j * blk, blk)
        kb = k_ref[0, pl.ds(start, blk), :]
        vb = v_ref[0, pl.ds(start, blk), :]
        s = _dot_nt(q, kb) - ck_ref[0, j]
        if diagonal:
            row = lax.broadcasted_iota(jnp.int32, (blk, blk), 0)
            col = lax.broadcasted_iota(jnp.int32, (blk, blk), 1)
            s = jnp.where(col <= row, s, -jnp.inf)
        m_prev = m_sc[...]
        m_new = jnp.maximum(m_prev, jnp.max(s, axis=-1, keepdims=True))
        alpha = jnp.exp2(m_prev - m_new)
        p = jnp.exp2(s - m_new)
        l_sc[...] = alpha * l_sc[...] + jnp.sum(p, axis=-1, keepdims=True)
        acc_sc[...] = alpha * acc_sc[...] + _dot(p.astype(BF), vb)
        m_sc[...] = m_new

    def body(j, carry):
        block(j, False)
        return carry

    lax.fori_loop(js_ref[(b * N_HEADS + h) * nq + i], i, body, 0)
    block(i, True)
    o_ref[0] = (acc_sc[...] / l_sc[...]).astype(BF)


def _fox_first_blocks(block_min, block_max, dead_gap):
    nb = block_min.shape[-1]
    j = jnp.arange(nb)
    upto = j[None, :] <= j[:, None]
    prefix_min = jnp.min(jnp.where(upto, block_min[..., None, :], jnp.inf), axis=-1)
    dead = (prefix_min[..., None, :] - block_max[..., :, None]) > dead_gap
    return jnp.sum(dead & (j[None, :] < j[:, None]), axis=-1).astype(jnp.int32)


def _fox_attention(q, kbf, vbf, logf, logit_bound, blk):
    B, T, D = q.shape
    nq = T // blk
    qx, kx, cum, bmin, bmax = _fox_prep(logf, logit_bound, blk)
    heads_first = lambda a: jnp.swapaxes(a[:, :, 0, :N_HEADS], 1, 2)
    use_fixed_reference = logit_bound < FIXED_REF_MAX_BOUND
    dead_gap = jnp.where(use_fixed_reference, (EXP2_FLUSH + 1.0) / LOG2E,
                         2.0 * logit_bound + EXP_UNDERFLOW + 1.0)
    first_blocks = _fox_first_blocks(heads_first(bmin), heads_first(bmax), dead_gap).reshape(-1)
    qspec = pl.BlockSpec((1, blk, HEAD_DIM), lambda b, h, i, js: (b, i, h))
    kvspec = pl.BlockSpec((1, T, HEAD_DIM), lambda b, h, i, js: (b, 0, h))
    out_shape = jax.ShapeDtypeStruct((B, T, D), BF)

    def fixed_reference():
        return pl.pallas_call(
            functools.partial(_fox_attn_kernel, blk=blk),
            grid_spec=pltpu.PrefetchScalarGridSpec(
                num_scalar_prefetch=1, grid=(B, N_HEADS, nq),
                in_specs=[qspec, qspec, kvspec, kvspec, kvspec], out_specs=qspec,
                scratch_shapes=[pltpu.VMEM((blk, 2 * HEAD_DIM), F32)]),
            out_shape=out_shape,
            compiler_params=_cparams("parallel", "parallel", "arbitrary"),
            name="fox_attention",
        )(first_blocks, q, qx, kbf, kx, vbf)

    def online():
        ck = jnp.swapaxes(cum[:, :, :N_HEADS], 1, 2) * LOG2E
        ckspec = pl.BlockSpec((1, nq, 1, blk), lambda b, h, i, js: (b * N_HEADS + h, 0, 0, 0))
        return pl.pallas_call(
            functools.partial(_fox_attn_online_kernel, blk=blk),
            grid_spec=pltpu.PrefetchScalarGridSpec(
                num_scalar_prefetch=1, grid=(B, N_HEADS, nq),
                in_specs=[qspec, kvspec, kvspec, ckspec], out_specs=qspec,
                scratch_shapes=[pltpu.VMEM((blk, 1), F32), pltpu.VMEM((blk, 1), F32),
                                pltpu.VMEM((blk, HEAD_DIM), F32)]),
            out_shape=out_shape,
            compiler_params=_cparams("parallel", "parallel", "arbitrary"),
            name="fox_attention_online",
        )(first_blocks, q, kbf, vbf, ck.reshape(B * N_HEADS, nq, 1, blk))

    return lax.cond(use_fixed_reference, fixed_reference, online)


def _cache_head(ref, h):
    tp = ref.shape[1]
    rows = ref.reshape(1, tp * N_HEADS, HEAD_DIM)
    return rows[0, pl.ds(h, tp, stride=N_HEADS), :].astype(BF)


def _fox_attn_past_kernel(q_ref, kn_ref, vn_ref, kp_ref, vp_ref, cp_ref, cn_ref, o_ref):
    tn = q_ref.shape[1]
    row = lax.broadcasted_iota(jnp.int32, (tn, tn), 0)
    col = lax.broadcasted_iota(jnp.int32, (tn, tn), 1)
    for h in range(N_HEADS):
        sl = slice(h * HEAD_DIM, (h + 1) * HEAD_DIM)
        q = q_ref[0, :, sl]
        sp = _dot_nt(q, _cache_head(kp_ref, h)) - cp_ref[0, h:h + 1, :]
        sn = _dot_nt(q, kn_ref[0, :, sl]) - cn_ref[0, h:h + 1, :]
        sn = jnp.where(col <= row, sn, -jnp.inf)
        m = jnp.maximum(jnp.max(sp, axis=-1, keepdims=True), jnp.max(sn, axis=-1, keepdims=True))
        pp = jnp.exp2(sp - m)
        pn = jnp.exp2(sn - m)
        l = jnp.sum(pp, axis=-1, keepdims=True) + jnp.sum(pn, axis=-1, keepdims=True)
        o = _dot(pp.astype(BF), _cache_head(vp_ref, h)) + _dot(pn.astype(BF), vn_ref[0, :, sl])
        o_ref[0, :, sl] = (o / l).astype(BF)


def _fox_attention_past(q, kn, vn, cache_k, cache_v, layer, cum_past, cum_new):
    B, Tn, D = q.shape
    Tp = cache_k.shape[2]
    new = pl.BlockSpec((1, Tn, D), lambda b: (b, 0, 0))
    past = pl.BlockSpec((None, 1, Tp, N_HEADS, HEAD_DIM), lambda b: (layer, b, 0, 0, 0))
    return pl.pallas_call(
        _fox_attn_past_kernel, grid=(B,),
        in_specs=[new, new, new, past, past,
                  pl.BlockSpec((1, N_HEADS, Tp), lambda b: (b, 0, 0)),
                  pl.BlockSpec((1, N_HEADS, Tn), lambda b: (b, 0, 0))],
        out_specs=new, out_shape=jax.ShapeDtypeStruct((B, Tn, D), BF),
        compiler_params=_cparams("parallel"),
        name="fox_attention_past",
    )(q, kn, vn, cache_k, cache_v, cum_past, cum_new)


def _neg_log2_rest(z2):
    zc = jnp.minimum(z2, 30.0)
    return jnp.log(1.0 + jnp.exp2(zc)) * LOG2E + (z2 - zc)


class _SbBlock:
    def __init__(self, blk):
        self.row = lax.broadcasted_iota(jnp.int32, (blk, blk), 0)
        self.col = lax.broadcasted_iota(jnp.int32, (blk, blk), 1)
        suffix = jnp.where(self.row >= self.col, -1.0, 0.0).astype(BF)
        self.suffix2 = jnp.concatenate([suffix, suffix], axis=0)

    def scores(self, q, kb, diagonal):
        z2 = _dot_nt(q, kb)
        nlr = _neg_log2_rest(z2)
        if diagonal:
            nlr = jnp.where(self.col < self.row, nlr, 0.0)
        hi, lo = _split_bf16(nlr)
        tail = _dot(jnp.concatenate([hi, lo], axis=-1), self.suffix2)
        return z2 + tail, tail[:, 0:1]

    def weighted(self, e, later, vb, diagonal):
        a = jnp.exp2(e + later)
        if diagonal:
            a = jnp.where(self.col < self.row, a, 0.0)
        return _dot(a.astype(BF), vb)


SB_HEADS_PER_STEP = 2
SB_FIRST_BLOCKS = 2


def _sb_attn_kernel(q_ref, k_ref, v_ref, o_ref, later_sc, acc_sc, *, blk):
    i = pl.program_id(2)
    tile = _SbBlock(blk)

    def head_cols(hh):
        return slice(hh * HEAD_DIM, (hh + 1) * HEAD_DIM)

    def kv(ref, j, hh):
        return ref[0, pl.ds(pl.multiple_of(j * blk, blk), blk), head_cols(hh)]

    def first_chunk(n_blocks):
        for hh in range(SB_HEADS_PER_STEP):
            q = q_ref[0, :, head_cols(hh)]
            later, acc = 0.0, None
            for back in range(n_blocks):
                e, total = tile.scores(q, kv(k_ref, i - back, hh), back == 0)
                part = tile.weighted(e, later, kv(v_ref, i - back, hh), back == 0)
                acc = part if acc is None else acc + part
                later = later + total
            acc_sc[hh] = acc
            later_sc[hh] = later

    for n_blocks in range(1, SB_FIRST_BLOCKS + 1):
        last = n_blocks == SB_FIRST_BLOCKS

        @pl.when((i >= n_blocks - 1) if last else (i == n_blocks - 1))
        def _(n_blocks=n_blocks):
            first_chunk(n_blocks)

    for hh in range(SB_HEADS_PER_STEP):
        q = q_ref[0, :, head_cols(hh)]

        def cond(carry):
            j, live = carry
            return jnp.logical_and(j >= 0, live > -EXP2_UNDERFLOW)

        def body(carry, hh=hh, q=q):
            j, _ = carry
            e, total = tile.scores(q, kv(k_ref, j, hh), False)
            later = later_sc[hh]
            acc_sc[hh] += tile.weighted(e, later, kv(v_ref, j, hh), False)
            later_sc[hh] = later + total
            return j - 1, jnp.max(later_sc[hh])

        lax.while_loop(cond, body, (i - SB_FIRST_BLOCKS, jnp.max(later_sc[hh])))
        o_ref[0, :, head_cols(hh)] = acc_sc[hh].astype(BF)


def _sb_attention(q, kbf, vbf, blk):
    B, T, D = q.shape
    width = SB_HEADS_PER_STEP * HEAD_DIM
    qspec = pl.BlockSpec((1, blk, width), lambda b, h, i: (b, i, h))
    kvspec = pl.BlockSpec((1, T, width), lambda b, h, i: (b, 0, h))
    return pl.pallas_call(
        functools.partial(_sb_attn_kernel, blk=blk),
        grid=(B, N_HEADS // SB_HEADS_PER_STEP, T // blk),
        in_specs=[qspec, kvspec, kvspec], out_specs=qspec,
        scratch_shapes=[pltpu.VMEM((SB_HEADS_PER_STEP, blk, 1), F32),
                        pltpu.VMEM((SB_HEADS_PER_STEP, blk, HEAD_DIM), F32)],
        out_shape=jax.ShapeDtypeStruct((B, T, D), BF),
        compiler_params=_cparams("parallel", "parallel", "arbitrary"),
        name="sb_attention",
    )(q, kbf, vbf)


def _sb_attn_past_kernel(q_ref, kn_ref, vn_ref, kp_ref, vp_ref, o_ref, zp_sc, lrp_sc, zn_sc, lrn_sc):
    tn = q_ref.shape[1]
    tp = kp_ref.shape[1]
    row = lax.broadcasted_iota(jnp.int32, (tn, tn), 0)
    col = lax.broadcasted_iota(jnp.int32, (tn, tn), 1)
    for h in range(N_HEADS):
        sl = slice(h * HEAD_DIM, (h + 1) * HEAD_DIM)
        rows = slice(h * tn, (h + 1) * tn)
        q = q_ref[0, :, sl]
        zp = _dot_nt(q, _cache_head(kp_ref, h))
        zn = _dot_nt(q, kn_ref[0, :, sl])
        zp_sc[rows, :] = zp
        lrp_sc[rows, :] = -_neg_log2_rest(zp)
        zn_sc[rows, :] = zn
        lrn_sc[rows, :] = jnp.where(col < row, -_neg_log2_rest(zn), 0.0)
    rp = lax.broadcasted_iota(jnp.int32, (tp, tp), 0)
    cp = lax.broadcasted_iota(jnp.int32, (tp, tp), 1)
    lrp = lrp_sc[...]
    lrn = lrn_sc[...]
    hi, lo = _split_bf16(lrp)
    later_p = (rp > cp).astype(BF)
    between_p = _dot(hi, later_p) + _dot(lo, later_p) + jnp.sum(lrn, axis=-1, keepdims=True)
    hi, lo = _split_bf16(lrn)
    later_n = (row > col).astype(BF)
    between_n = _dot(hi, later_n) + _dot(lo, later_n)
    ap = jnp.exp2(zp_sc[...] + lrp + between_p).astype(BF)
    an = jnp.exp2(zn_sc[...] + lrn + between_n)
    for h in range(N_HEADS):
        sl = slice(h * HEAD_DIM, (h + 1) * HEAD_DIM)
        rows = slice(h * tn, (h + 1) * tn)
        a_new = jnp.where(col < row, an[rows, :], 0.0).astype(BF)
        o = _dot(ap[rows, :], _cache_head(vp_ref, h)) + _dot(a_new, vn_ref[0, :, sl])
        o_ref[0, :, sl] = o.astype(BF)


def _sb_attention_past(q, kn, vn, cache_k, cache_v, layer):
    B, Tn, D = q.shape
    Tp = cache_k.shape[2]
    new = pl.BlockSpec((1, Tn, D), lambda b: (b, 0, 0))
    past = pl.BlockSpec((None, 1, Tp, N_HEADS, HEAD_DIM), lambda b: (layer, b, 0, 0, 0))
    return pl.pallas_call(
        _sb_attn_past_kernel, grid=(B,),
        in_specs=[new, new, new, past, past], out_specs=new,
        scratch_shapes=[pltpu.VMEM((N_HEADS * Tn, Tp), F32), pltpu.VMEM((N_HEADS * Tn, Tp), F32),
                        pltpu.VMEM((N_HEADS * Tn, Tn), F32), pltpu.VMEM((N_HEADS * Tn, Tn), F32)],
        out_shape=jax.ShapeDtypeStruct((B, Tn, D), BF),
        compiler_params=_cparams("parallel"),
        name="sb_attention_past",
    )(q, kn, vn, cache_k, cache_v)


def _top2_gates(logits):
    lane = lax.broadcasted_iota(jnp.int32, logits.shape, 1)
    logits = jnp.where(lane < N_EXPERTS, logits, -jnp.inf)
    v1 = jnp.max(logits, axis=-1, keepdims=True)
    i1 = jnp.min(jnp.where(logits == v1, lane, LANES), axis=-1, keepdims=True)
    rest = jnp.where(lane == i1, -jnp.inf, logits)
    v2 = jnp.max(rest, axis=-1, keepdims=True)
    i2 = jnp.min(jnp.where(rest == v2, lane, LANES), axis=-1, keepdims=True)
    e2 = jnp.exp(v2 - v1)
    return jnp.where(lane == i1, 1.0 / (1.0 + e2), 0.0) + jnp.where(lane == i2, e2 / (1.0 + e2), 0.0)


def _ffn_kernel(*refs, moe, mixed):
    refs = list(refs)
    x_ref, g_ref, sh_ref, sc_ref, gate_ref = refs[:5]
    del refs[:5]
    if mixed:
        a_ref, wo_ref, mix_gate_ref = refs[:3]
        del refs[:3]
    if moe:
        wr_hi_ref, wr_lo_ref = refs[:2]
        del refs[:2]
    wg_ref, wu_ref, wd_ref, o_ref, h_sc, acc_sc = refs[:6]
    f = pl.program_id(2)

    @pl.when(f == 0)
    def _():
        x = x_ref[0]
        if mixed:
            x = x + mix_gate_ref[0] * _dot(a_ref[0], wo_ref[...])
        o_ref[0] = x
        h = _normmod(x, g_ref[...], sh_ref[0], sc_ref[0])
        h_sc[...] = h.astype(BF)
        acc_sc[...] = jnp.zeros_like(acc_sc)
        if moe:
            refs[6][...] = _top2_gates(_dot_3pass(h, wr_hi_ref[...], wr_lo_ref[...]))

    hb = h_sc[...]
    g = _dot(hb, wg_ref[...])
    u = _dot(hb, wu_ref[...])
    a = g * jax.nn.sigmoid(g) * u
    if moe:
        gates = refs[6][...]
        lane = lax.broadcasted_iota(jnp.int32, gates.shape, 1)
        a = a * jnp.sum(jnp.where(lane == f, gates, 0.0), axis=-1, keepdims=True)
    acc_sc[...] += _dot(a.astype(BF), wd_ref[...])

    @pl.when(f == pl.num_programs(2) - 1)
    def _():
        o_ref[0] = o_ref[0] + gate_ref[0] * acc_sc[...]


def _ffn(x, gain, shift, scale, gate, w_gate, w_up, w_down, w_router=None, mixer=None):
    B, T, D = x.shape
    tm = min(T, FFN_ROW_TILE)
    moe = w_router is not None
    mixed = mixer is not None
    row = pl.BlockSpec((1, tm, D), lambda b, i, f: (b, i, 0))
    in_specs = [row, _const_spec((1, D)), _mod_spec(shift, tm), _mod_spec(scale, tm), _mod_spec(gate, tm)]
    args = [x, gain.reshape(1, D), shift, scale, gate]
    if mixed:
        a, w_out, mix_gate = mixer
        in_specs += [pl.BlockSpec((1, tm, a.shape[-1]), lambda b, i, f: (b, i, 0)),
                     _const_spec(w_out.shape), _mod_spec(mix_gate, tm)]
        args += [a, w_out, mix_gate]
    scratch = [pltpu.VMEM((tm, D), BF), pltpu.VMEM((tm, D), F32)]
    if moe:
        nf, _, tf = w_gate.shape
        in_specs += [_const_spec((D, LANES)), _const_spec((D, LANES)),
                     pl.BlockSpec((None, D, tf), lambda b, i, f: (f, 0, 0)),
                     pl.BlockSpec((None, D, tf), lambda b, i, f: (f, 0, 0)),
                     pl.BlockSpec((None, tf, D), lambda b, i, f: (f, 0, 0))]
        args += list(_split_bf16(jnp.pad(w_router, ((0, 0), (0, LANES - N_EXPERTS)))))
        scratch.append(pltpu.VMEM((tm, LANES), F32))
    else:
        tf = FFN_COL_TILE
        nf = w_gate.shape[1] // tf
        in_specs += [pl.BlockSpec((D, tf), lambda b, i, f: (0, f)),
                     pl.BlockSpec((D, tf), lambda b, i, f: (0, f)),
                     pl.BlockSpec((tf, D), lambda b, i, f: (f, 0))]
    args += [w_gate, w_up, w_down]
    return pl.pallas_call(
        functools.partial(_ffn_kernel, moe=moe, mixed=mixed),
        grid=(B, T // tm, nf),
        in_specs=in_specs, out_specs=row, scratch_shapes=scratch,
        out_shape=jax.ShapeDtypeStruct((B, T, D), F32),
        compiler_params=_cparams("parallel", "parallel", "arbitrary"),
        name="moe_ffn" if moe else "dense_ffn",
    )(*args)


def _lru_proj_kernel(x_ref, g_ref, sh_ref, sc_ref, w_ref, gate_ref, xr_ref, tail_ref, *, tail_rows):
    h = _normmod(x_ref[0], g_ref[...], sh_ref[0], sc_ref[0])
    hb = h.astype(BF)
    tm = hb.shape[0]
    for c in range(2 * D_RNN // MXU_N):
        acc = _dot(hb, w_ref[:, c * MXU_N:(c + 1) * MXU_N])
        if c * MXU_N < D_RNN:
            gate_ref[0, :, c * MXU_N:(c + 1) * MXU_N] = jax.nn.gelu(acc).astype(BF)
        else:
            sl = slice(c * MXU_N - D_RNN, (c + 1) * MXU_N - D_RNN)
            xr_ref[0, :, sl] = acc.astype(BF)
            tail_ref[0, :, sl] = acc[tm - tail_rows:, :]


def _lru_proj(x, gain, shift, scale, w_in, all_rows_f32):
    B, T, D = x.shape
    tm = min(T, ROW_TILE)
    tail_rows = tm if all_rows_f32 else SUBLANES
    tail_spec = pl.BlockSpec((1, tail_rows, D_RNN), (lambda b, i: (b, i, 0)) if all_rows_f32
                                       else (lambda b, i: (b, 0, 0)))
    row = pl.BlockSpec((1, tm, D), lambda b, i: (b, i, 0))
    rnn = pl.BlockSpec((1, tm, D_RNN), lambda b, i: (b, i, 0))
    return pl.pallas_call(
        functools.partial(_lru_proj_kernel, tail_rows=tail_rows), grid=(B, T // tm),
        in_specs=[row, _const_spec((1, D)), _mod_spec(shift, tm), _mod_spec(scale, tm),
                  _const_spec((D, 2 * D_RNN))],
        out_specs=[rnn, rnn, tail_spec],
        out_shape=[jax.ShapeDtypeStruct((B, T, D_RNN), BF), jax.ShapeDtypeStruct((B, T, D_RNN), BF),
                   jax.ShapeDtypeStruct((B, T if all_rows_f32 else SUBLANES, D_RNN), F32)],
        compiler_params=_cparams("parallel", "arbitrary"),
        name="lru_proj",
    )(x, gain.reshape(1, D), shift, scale, w_in)


def _lru_core_kernel(gg_ref, xr_ref, x_ref, gate_ref, h0_ref, prev_ref, cw_ref, cb_ref, wri_ref, bri_ref,
                     lam_ref, wo_ref, o_ref, hl_ref, xp_sc, a_sc, b_sc, hs_sc, h_sc):
    i = pl.program_id(1)
    tm = xr_ref.shape[1]

    @pl.when(i == 0)
    def _():
        xp_sc[0:SUBLANES, :] = prev_ref[0]
        h_sc[...] = h0_ref[0]

    xp_sc[SUBLANES:, :] = xr_ref[0].astype(F32)
    xc = cb_ref[...]
    for j in range(CONV_W):
        off = SUBLANES - (CONV_W - 1) + j
        xc = xc + cw_ref[j:j + 1, :] * xp_sc[off:off + tm, :]
    xp_sc[0:SUBLANES, :] = xp_sc[tm:tm + SUBLANES, :]

    xcb = xc.astype(BF)
    neg_decay = -LRU_C * _softplus(-lam_ref[...])
    for s in range(D_RNN // RNN_SUPER):
        sl = slice(s * RNN_SUPER, (s + 1) * RNN_SUPER)
        z = _dot(xcb[:, sl], wri_ref[s])
        r = jax.nn.sigmoid(z[:, :RNN_SUPER] + bri_ref[0:1, sl])
        g = jax.nn.sigmoid(z[:, RNN_SUPER:] + bri_ref[1:2, sl])
        log_a = neg_decay[:, sl] * r
        y = 2.0 * log_a
        a_sc[:, sl] = jnp.exp(log_a)
        b_sc[:, sl] = jnp.sqrt(-jnp.tanh(log_a) * (jnp.exp(y) + 1.0)) * (g * xc[:, sl])

    def step(t, h):
        h = a_sc[pl.ds(t, 1), :] * h + b_sc[pl.ds(t, 1), :]
        hs_sc[pl.ds(t, 1), :] = h
        return h

    h_last = lax.fori_loop(0, tm, step, h_sc[...], unroll=8)
    h_sc[...] = h_last
    hl_ref[0] = h_last
    y = (gg_ref[0].astype(F32) * hs_sc[...]).astype(BF)
    o_ref[0] = x_ref[0] + gate_ref[0] * _dot(y, wo_ref[...])


def _block_diag(w):
    n_super = D_RNN // RNN_SUPER
    per = N_RNN_BLOCKS // n_super
    w = w.reshape(n_super, per, w.shape[1], w.shape[2])
    return jnp.einsum('sbij,bc->sbicj', w, jnp.eye(per, dtype=w.dtype)).reshape(n_super, RNN_SUPER, RNN_SUPER)


def _lru_core(gg, xr, x, gate, h0, prev, conv_w, conv_b, w_r, b_r, w_i, b_i, lam, w_out):
    B, T, D = x.shape
    tm = min(T, ROW_TILE)
    w_ri = jnp.concatenate([_block_diag(w_r), _block_diag(w_i)], axis=-1).astype(BF)
    b_ri = jnp.stack([b_r, b_i])
    row = pl.BlockSpec((1, tm, D), lambda b, i: (b, i, 0))
    rnn = pl.BlockSpec((1, tm, D_RNN), lambda b, i: (b, i, 0))
    n_super = D_RNN // RNN_SUPER
    return pl.pallas_call(
        _lru_core_kernel, grid=(B, T // tm),
        in_specs=[rnn, rnn, row, _mod_spec(gate, tm),
                  pl.BlockSpec((1, 1, D_RNN), lambda b, i: (b, 0, 0)),
                  pl.BlockSpec((1, SUBLANES, D_RNN), lambda b, i: (b, 0, 0)),
                  _const_spec((CONV_W, D_RNN)), _const_spec((1, D_RNN)),
                  _const_spec((n_super, RNN_SUPER, 2 * RNN_SUPER)), _const_spec((2, D_RNN)),
                  _const_spec((1, D_RNN)), _const_spec((D_RNN, D))],
        out_specs=[row, pl.BlockSpec((1, 1, D_RNN), lambda b, i: (b, 0, 0))],
        out_shape=[jax.ShapeDtypeStruct((B, T, D), F32), jax.ShapeDtypeStruct((B, 1, D_RNN), F32)],
        scratch_shapes=[pltpu.VMEM((tm + SUBLANES, D_RNN), F32), pltpu.VMEM((tm, D_RNN), F32),
                        pltpu.VMEM((tm, D_RNN), F32), pltpu.VMEM((tm, D_RNN), F32),
                        pltpu.VMEM((1, D_RNN), F32)],
        compiler_params=_cparams("parallel", "arbitrary"),
        name="lru_core",
    )(gg, xr, x, gate, h0, prev, conv_w, conv_b.reshape(1, D_RNN), w_ri, b_ri, lam.reshape(1, D_RNN), w_out)


def _trunk(x, mod, W, past):
    B, T, D = x.shape
    depth = mod.shape[0]
    has_past = past is not None
    flat = (lambda a: a.reshape(1, B * T, a.shape[-1])) if has_past else (lambda a: a)
    unflat = (lambda a: a.reshape(B, T, a.shape[-1])) if has_past else (lambda a: a)
    n_fox = len(range(0, depth, N_MIXERS))
    n_sb = len(range(2, depth, N_MIXERS))
    fox_kv = sb_kv = None
    fox_logf, lru_new = [], []
    for i in range(depth):
        per_batch = [m[:, None, :] for m in jnp.split(mod[i], 6, axis=-1)]
        if has_past:
            per_row = [jnp.broadcast_to(m, (B, T, D)).reshape(1, B * T, D) for m in per_batch]
        else:
            per_row = per_batch
        shift1, scale1, gate1, shift2, scale2, gate2 = per_row
        kind, j = i % N_MIXERS, i // N_MIXERS
        if kind == 0:
            fox_params = (W['fox_w_f'][j], W['fox_b_f'][j], W['fox_q_norm'][j], W['fox_k_norm'][j])
            q, k32, kbf, v32, vbf, logf = _qkv_proj(flat(x), W['norm_mix'][i], shift1, scale1,
                                                    W['fox_w_qkv'][j], fox_params, n_fox, j, fox_kv)
            fox_kv = (k32, v32)
            q, kbf, vbf, logf = (unflat(a) for a in (q, kbf, vbf, logf))
            fox_logf.append(logf[..., :N_HEADS])
            if has_past:
                cache_k, cache_v, cache_lf = past['fox']
                Tp = cache_k.shape[2]
                lf_all = jnp.concatenate([jnp.swapaxes(cache_lf[j], 1, 2),
                                          jnp.swapaxes(fox_logf[-1], 1, 2)], axis=-1)
                pad = (-lf_all.shape[-1]) % LANES
                cum = _cumsum_flat(jnp.pad(lf_all, ((0, 0), (0, 0), (0, pad))).reshape(B * N_HEADS, -1))
                cum = cum.reshape(B, N_HEADS, -1) * LOG2E
                o = _fox_attention_past(q, kbf, vbf, cache_k, cache_v, j,
                                        cum[..., :Tp], cum[..., Tp:Tp + T])
            else:
                bound = 1.02 * HEAD_DIM ** 0.5 * jnp.max(jnp.abs(W['fox_q_norm'][j])) \
                    * jnp.max(jnp.abs(W['fox_k_norm'][j]))
                o = _fox_attention(q, kbf, vbf, logf, bound, min(T, FOX_BLOCK))
            mixer = (flat(o), W['fox_w_out'][j], gate1)
        elif kind == 1:
            gg, xr, tail = _lru_proj(flat(x), W['norm_mix'][i], shift1, scale1, W['lru_w_in'][j], has_past)
            gg, xr, tail = unflat(gg), unflat(xr), unflat(tail) if has_past else tail
            if has_past:
                state_h, state_conv = past['lru']
                h0 = state_h[j][:, None, :]
                prev = jnp.pad(state_conv[j], ((0, 0), (SUBLANES - (CONV_W - 1), 0), (0, 0)))
            else:
                h0 = jnp.zeros((B, 1, D_RNN), F32)
                prev = jnp.zeros((B, SUBLANES, D_RNN), F32)
            x, h_last = _lru_core(gg, xr, x, per_batch[2], h0, prev, W['lru_conv_w'][j], W['lru_conv_b'][j],
                                  W['lru_w_r'][j], W['lru_b_r'][j], W['lru_w_i'][j], W['lru_b_i'][j],
                                  W['lru_lambda'][j], W['lru_w_out'][j])
            lru_new.append((h_last[:, 0], tail[:, tail.shape[1] - (CONV_W - 1):]))
            mixer = None
        else:
            q, k32, kbf, v32, vbf = _qkv_proj(flat(x), W['norm_mix'][i], shift1, scale1, W['sb_w_in'][j],
                                              None, n_sb, j, sb_kv)
            sb_kv = (k32, v32)
            q, kbf, vbf = (unflat(a) for a in (q, kbf, vbf))
            if has_past:
                cache_k, cache_v = past['sb']
                o = _sb_attention_past(q, kbf, vbf, cache_k, cache_v, j)
            else:
                o = _sb_attention(q, kbf, vbf, min(T, SB_BLOCK))
            mixer = (flat(o), W['sb_w_out'][j], gate1)
        if i % 2 == 0:
            x = unflat(_ffn(flat(x), W['norm_ffn'][i], shift2, scale2, gate2,
                            W['ffn_w_gate'][i // 2], W['ffn_w_up'][i // 2], W['ffn_w_down'][i // 2],
                            None, mixer))
        else:
            x = unflat(_ffn(flat(x), W['norm_ffn'][i], shift2, scale2, gate2,
                            W['moe_w_gate'][i // 2], W['moe_w_up'][i // 2], W['moe_w_down'][i // 2],
                            W['moe_w_router'][i // 2], mixer))

    def heads(state):
        return state.reshape(state.shape[0], B, T, N_HEADS, HEAD_DIM)

    fox_state = [heads(fox_kv[0]), heads(fox_kv[1]), jnp.stack(fox_logf)]
    lru_state = [jnp.stack([s[n] for s in lru_new]) for n in range(2)]
    sb_state = [heads(sb_kv[0]), heads(sb_kv[1])]
    return x, fox_state, lru_state, sb_state


MATMUL_WEIGHTS = ('fox_w_qkv', 'fox_w_out', 'lru_w_in', 'lru_w_out', 'sb_w_in', 'sb_w_out',
                  'ffn_w_gate', 'ffn_w_up', 'ffn_w_down', 'moe_w_gate', 'moe_w_up', 'moe_w_down')


def _prepare_weights(W):
    W = dict(W)
    w_in = W.pop('fox_w_in')
    W['fox_w_qkv'] = w_in[:, :, :3 * D_MODEL]
    W['fox_w_f'] = w_in[:, :, 3 * D_MODEL:]
    for name in MATMUL_WEIGHTS:
        W[name] = W[name].astype(BF)
    return W


def _forward(W, x_prompt, c_prompt, x_sample, c_sample, past):
    W = _prepare_weights(W)
    n_prompt, n_sample = c_prompt.shape[0], c_sample.shape[0]
    c_all = jnp.concatenate([c_prompt, c_sample], axis=0)
    c_all = jnp.pad(c_all, ((0, (-c_all.shape[0]) % (2 * SUBLANES)), (0, 0)))
    mod = _ada(c_all, W['w_ada'], W['b_ada'])
    y_p, fox_p, lru_p, sb_p = _trunk(x_prompt, mod[:, :n_prompt], W, None)
    y_s, fox_s, lru_s, sb_s = _trunk(x_sample, mod[:, n_prompt:n_prompt + n_sample], W, past)

    return (y_p, y_s, *fox_p, *fox_s, *lru_p, *lru_s, *sb_p, *sb_s)


def kernel(x_prompt, x_sample, cache_fox_k, cache_fox_v, cache_fox_logf, state_lru_h, state_lru_conv,
           cache_sb_k, cache_sb_v, c_prompt, c_sample, norm_mix, norm_ffn, w_ada, b_ada,
           fox_w_in, fox_b_f, fox_q_norm, fox_k_norm, fox_w_out,
           lru_w_in, lru_conv_w, lru_conv_b, lru_w_r, lru_b_r, lru_w_i, lru_b_i, lru_lambda, lru_w_out,
           sb_w_in, sb_w_out, ffn_w_gate, ffn_w_up, ffn_w_down,
           moe_w_router, moe_w_gate, moe_w_up, moe_w_down):
    W = {
        'norm_mix': norm_mix, 'norm_ffn': norm_ffn, 'w_ada': w_ada, 'b_ada': b_ada,
        'fox_w_in': fox_w_in, 'fox_b_f': fox_b_f, 'fox_q_norm': fox_q_norm,
        'fox_k_norm': fox_k_norm, 'fox_w_out': fox_w_out,
        'lru_w_in': lru_w_in, 'lru_conv_w': lru_conv_w, 'lru_conv_b': lru_conv_b,
        'lru_w_r': lru_w_r, 'lru_b_r': lru_b_r, 'lru_w_i': lru_w_i, 'lru_b_i': lru_b_i,
        'lru_lambda': lru_lambda, 'lru_w_out': lru_w_out,
        'sb_w_in': sb_w_in, 'sb_w_out': sb_w_out,
        'ffn_w_gate': ffn_w_gate, 'ffn_w_up': ffn_w_up, 'ffn_w_down': ffn_w_down,
        'moe_w_router': moe_w_router, 'moe_w_gate': moe_w_gate, 'moe_w_up': moe_w_up,
        'moe_w_down': moe_w_down,
    }
    past = {'fox': (cache_fox_k, cache_fox_v, cache_fox_logf), 'lru': (state_lru_h, state_lru_conv),
            'sb': (cache_sb_k, cache_sb_v)}
    return _forward(W, x_prompt, c_prompt, x_sample, c_sample, past)
```
